```python
import math
import jax, jax.numpy as jnp
from jax import lax
import numpy as np

D_MODEL = 1024
BATCH = 8
SEQ = 4096
DEPTH = 2
DEC_BATCH = 8
DEC_SEQ = 16
PAST_LEN = 2048

CHUNK = 64
HEAD_DIM = 64
FOX_HEADS = 8
SB_HEADS = 8
FOX_W = FOX_HEADS * HEAD_DIM
SB_W = SB_HEADS * HEAD_DIM
MIX_W = FOX_W + SB_W
IN_W = 3 * FOX_W + FOX_HEADS + 3 * SB_W
MEM_TOKENS = 256
MEM_HEADS = 4
MEM_HEAD_DIM = D_MODEL // MEM_HEADS
MEM_W = MEM_HEADS * MEM_HEAD_DIM
D_FF = 4 * D_MODEL
QBLK = 128
EPS = 1e-6
NEG = -1e30

kernel_name = "hybrid_fox_stickbreak_streaming_step"


def rmsnorm(x, g):
    xf = x.astype(jnp.float32)
    y = xf * lax.rsqrt(jnp.mean(xf * xf, axis=-1, keepdims=True) + EPS)
    return (y * g.astype(jnp.float32)).astype(x.dtype)


def _block_size(tq):
    return QBLK if tq % QBLK == 0 else tq


def _split_blocks(a, blk):
    b, t = a.shape[:2]
    return jnp.moveaxis(a.reshape((b, t // blk, blk) + a.shape[2:]), 1, 0)


def _merge_blocks(a):
    nb, b, blk = a.shape[:3]
    return jnp.moveaxis(a, 0, 1).reshape((b, nb * blk) + a.shape[3:])


def fox_attention(q, k, v, logf_all):
    B, Tq, H, hd = q.shape
    Tk = k.shape[1]
    off = Tk - Tq
    F = jnp.cumsum(logf_all.astype(jnp.float32), axis=1)
    FkT = jnp.transpose(F, (0, 2, 1))
    Fq = F[:, off:]
    blk = _block_size(Tq)
    kpos = jnp.arange(Tk)
    scale = 1.0 / math.sqrt(hd)

    def one_block(args):
        qb, fqb, start = args
        s = jnp.einsum('bqhd,bkhd->bhqk', qb, k).astype(jnp.float32) * scale
        s = s + jnp.transpose(fqb, (0, 2, 1))[..., None] - FkT[:, :, None, :]
        qpos = off + start + jnp.arange(blk)
        s = jnp.where(kpos[None, :] <= qpos[:, None], s, NEG)
        p = jax.nn.softmax(s, axis=-1)
        return jnp.einsum('bhqk,bkhd->bqhd', p.astype(v.dtype), v)

    starts = jnp.arange(Tq // blk) * blk
    out = lax.map(one_block, (_split_blocks(q, blk), _split_blocks(Fq, blk), starts))
    return _merge_blocks(out)


def sb_attention(q, k, v):
    B, Tq, H, hd = q.shape
    Tk = k.shape[1]
    off = Tk - Tq
    blk = _block_size(Tq)
    kpos = jnp.arange(Tk)
    scale = 1.0 / math.sqrt(hd)

    def one_block(args):
        qb, start = args
        z = jnp.einsum('bqhd,bkhd->bhqk', qb, k).astype(jnp.float32) * scale
        qpos = off + start + jnp.arange(blk)
        strict = kpos[None, :] < qpos[:, None]
        log_1m = jnp.where(strict, jax.nn.log_sigmoid(-z), 0.0)
        rest = lax.cumsum(log_1m, axis=3, reverse=True) - log_1m
        a = jnp.where(strict, jnp.exp(jax.nn.log_sigmoid(z) + rest), 0.0)
        return jnp.einsum('bhqk,bkhd->bqhd', a.astype(v.dtype), v)

    starts = jnp.arange(Tq // blk) * blk
    out = lax.map(one_block, (_split_blocks(q, blk), starts))
    return _merge_blocks(out)


def memory_kv(mem, g_mem, w_mk, w_mv, g_mk):
    B, M, _ = mem.shape
    m = rmsnorm(mem, g_mem)
    k = rmsnorm((m @ w_mk).reshape(B, M, MEM_HEADS, MEM_HEAD_DIM), g_mk)
    v = (m @ w_mv).reshape(B, M, MEM_HEADS, MEM_HEAD_DIM)
    return k, v


def cross_attention(h, mem_k, mem_v, w_mq, g_mq, w_mo):
    B, T, _ = h.shape
    q = rmsnorm((h @ w_mq).reshape(B, T, MEM_HEADS, MEM_HEAD_DIM), g_mq)
    s = jnp.einsum('bqhd,bkhd->bhqk', q, mem_k.astype(q.dtype)).astype(jnp.float32)
    p = jax.nn.softmax(s * (1.0 / math.sqrt(MEM_HEAD_DIM)), axis=-1)
    o = jnp.einsum('bhqk,bkhd->bqhd', p.astype(h.dtype), mem_v.astype(h.dtype))
    return o.reshape(B, T, MEM_W) @ w_mo


def _layer(x, mem_k, mem_v, past, p):
    B, T, _ = x.shape
    h = rmsnorm(x, p['g_mix'])
    proj = h @ p['w_in']
    q_f = proj[..., 0:FOX_W]
    k_f = proj[..., FOX_W:2 * FOX_W]
    v_f = proj[..., 2 * FOX_W:3 * FOX_W]
    f_lin = proj[..., 3 * FOX_W:3 * FOX_W + FOX_HEADS]
    o_sb = 3 * FOX_W + FOX_HEADS
    q_s = proj[..., o_sb:o_sb + SB_W]
    k_s = proj[..., o_sb + SB_W:o_sb + 2 * SB_W]
    v_s = proj[..., o_sb + 2 * SB_W:o_sb + 3 * SB_W]

    q_f = rmsnorm(q_f.reshape(B, T, FOX_HEADS, HEAD_DIM), p['g_fox_q'])
    k_f = rmsnorm(k_f.reshape(B, T, FOX_HEADS, HEAD_DIM), p['g_fox_k'])
    v_f = v_f.reshape(B, T, FOX_HEADS, HEAD_DIM)
    logf = jax.nn.log_sigmoid(f_lin.astype(jnp.float32) + p['b_forget'].astype(jnp.float32))
    q_s = q_s.reshape(B, T, SB_HEADS, HEAD_DIM)
    k_s = k_s.reshape(B, T, SB_HEADS, HEAD_DIM)
    v_s = v_s.reshape(B, T, SB_HEADS, HEAD_DIM)

    if past is None:
        kf_all, vf_all, lf_all, ks_all, vs_all = k_f, v_f, logf, k_s, v_s
    else:
        pk_f, pv_f, plf, pk_s, pv_s = past
        kf_all = jnp.concatenate([pk_f, k_f], axis=1)
        vf_all = jnp.concatenate([pv_f, v_f], axis=1)
        lf_all = jnp.concatenate([plf.astype(jnp.float32), logf], axis=1)
        ks_all = jnp.concatenate([pk_s, k_s], axis=1)
        vs_all = jnp.concatenate([pv_s, v_s], axis=1)

    o_f = fox_attention(q_f, kf_all, vf_all, lf_all)
    o_s = sb_attention(q_s, ks_all, vs_all)
    o = jnp.concatenate([rmsnorm(o_f.reshape(B, T, FOX_W), p['g_out_fox']),
                         rmsnorm(o_s.reshape(B, T, SB_W), p['g_out_sb'])], axis=-1)
    x = x + o @ p['w_out']
    x = x + cross_attention(rmsnorm(x, p['g_cross']), mem_k, mem_v, p['w_mq'], p['g_mq'], p['w_mo'])
    h = rmsnorm(x, p['g_ffn'])
    x = x + jnp.square(jax.nn.relu(h @ p['w_ff1'])) @ p['w_ff2']
    return x, (k_f, v_f, logf, k_s, v_s)


def setup_inputs(seed: int = 0) -> dict:
    key = jax.random.key(seed)
    ks = iter(jax.random.split(key, 40))

    def nrm(shape, scale=1.0):
        return jax.random.normal(next(ks), shape, jnp.float32) * scale

    def gain(shape):
        return 1.0 + 0.02 * nrm(shape)

    L = DEPTH
    return {
        "x_prompt": nrm((BATCH, SEQ, D_MODEL)),
        "x_sample": nrm((DEC_BATCH, DEC_SEQ, D_MODEL)),
        "mem_prompt": nrm((BATCH, MEM_TOKENS, D_MODEL)),
        "cache_fox_k": nrm((L, DEC_BATCH, PAST_LEN, FOX_HEADS, HEAD_DIM)),
        "cache_fox_v": nrm((L, DEC_BATCH, PAST_LEN, FOX_HEADS, HEAD_DIM)),
        "cache_fox_logf": jax.nn.log_sigmoid(nrm((L, DEC_BATCH, PAST_LEN, FOX_HEADS)) + 1.0),
        "cache_sb_k": nrm((L, DEC_BATCH, PAST_LEN, SB_HEADS, HEAD_DIM)),
        "cache_sb_v": nrm((L, DEC_BATCH, PAST_LEN, SB_HEADS, HEAD_DIM)),
        "cache_mem_k": nrm((L, DEC_BATCH, MEM_TOKENS, MEM_HEADS, MEM_HEAD_DIM)),
        "cache_mem_v": nrm((L, DEC_BATCH, MEM_TOKENS, MEM_HEADS, MEM_HEAD_DIM)),
        "g_mix": gain((L, D_MODEL)),
        "w_in": nrm((L, D_MODEL, IN_W), D_MODEL ** -0.5),
        "b_forget": 1.0 + 0.5 * nrm((L, FOX_HEADS)),
        "g_fox_q": gain((L, HEAD_DIM)),
        "g_fox_k": gain((L, HEAD_DIM)),
        "g_out_fox": gain((L, FOX_W)),
        "g_out_sb": gain((L, SB_W)),
        "w_out": nrm((L, MIX_W, D_MODEL), MIX_W ** -0.5),
        "g_cross": gain((L, D_MODEL)),
        "g_mem": gain((L, D_MODEL)),
        "w_mq": nrm((L, D_MODEL, MEM_W), D_MODEL ** -0.5),
        "w_mk": nrm((L, D_MODEL, MEM_W), D_MODEL ** -0.5),
        "w_mv": nrm((L, D_MODEL, MEM_W), D_MODEL ** -0.5),
        "g_mq": gain((L, MEM_HEAD_DIM)),
        "g_mk": gain((L, MEM_HEAD_DIM)),
        "w_mo": nrm((L, MEM_W, D_MODEL), MEM_W ** -0.5),
        "g_ffn": gain((L, D_MODEL)),
        "w_ff1": nrm((L, D_MODEL, D_FF), D_MODEL ** -0.5),
        "w_ff2": nrm((L, D_FF, D_MODEL), D_FF ** -0.5),
    }


def reference(x_prompt, x_sample, mem_prompt,
              cache_fox_k, cache_fox_v, cache_fox_logf, cache_sb_k, cache_sb_v,
              cache_mem_k, cache_mem_v,
              g_mix, w_in, b_forget, g_fox_q, g_fox_k, g_out_fox, g_out_sb, w_out,
              g_cross, g_mem, w_mq, w_mk, w_mv, g_mq, g_mk, w_mo,
              g_ffn, w_ff1, w_ff2):
    xp = x_prompt
    xs = x_sample
    p_fk, p_fv, p_lf, p_sk, p_sv, p_mk, p_mv = [], [], [], [], [], [], []
    s_fk, s_fv, s_lf, s_sk, s_sv = [], [], [], [], []
    for l in range(DEPTH):
        p = dict(g_mix=g_mix[l], w_in=w_in[l], b_forget=b_forget[l],
                 g_fox_q=g_fox_q[l], g_fox_k=g_fox_k[l],
                 g_out_fox=g_out_fox[l], g_out_sb=g_out_sb[l], w_out=w_out[l],
                 g_cross=g_cross[l], w_mq=w_mq[l], g_mq=g_mq[l], w_mo=w_mo[l],
                 g_ffn=g_ffn[l], w_ff1=w_ff1[l], w_ff2=w_ff2[l])
        mk, mv = memory_kv(mem_prompt, g_mem[l], w_mk[l], w_mv[l], g_mk[l])
        xp, (kf, vf, lf, ksb, vsb) = _layer(xp, mk, mv, None, p)
        p_fk.append(kf); p_fv.append(vf); p_lf.append(lf)
        p_sk.append(ksb); p_sv.append(vsb); p_mk.append(mk); p_mv.append(mv)
        past = (cache_fox_k[l], cache_fox_v[l], cache_fox_logf[l], cache_sb_k[l], cache_sb_v[l])
        xs, (kf, vf, lf, ksb, vsb) = _layer(xs, cache_mem_k[l], cache_mem_v[l], past, p)
        s_fk.append(kf); s_fv.append(vf); s_lf.append(lf)
        s_sk.append(ksb); s_sv.append(vsb)
    return (xp, xs,
            jnp.stack(p_fk), jnp.stack(p_fv), jnp.stack(p_lf),
            jnp.stack(p_sk), jnp.stack(p_sv), jnp.stack(p_mk), jnp.stack(p_mv),
            jnp.stack(s_fk), jnp.stack(s_fv), jnp.stack(s_lf),
            jnp.stack(s_sk), jnp.stack(s_sv))
```

```python
import functools

import jax
import jax.numpy as jnp
from jax import lax
from jax.experimental import pallas as pl
from jax.experimental.pallas import tpu as pltpu

EPS = 1e-6
NEG = -1e30
HEAD_DIM = 64
MEM_HEADS = 4
LANES = 128
VMEM_LIMIT_BYTES = 56 * 2**20
F32 = jnp.float32
BF16 = jnp.bfloat16


def _dot(a, b):
    return jnp.dot(a, b, preferred_element_type=F32)


def _dot_nt(a, b):
    return lax.dot_general(a, b, (((1,), (1,)), ((), ())), preferred_element_type=F32)


def _rms(x, g):
    return x * lax.rsqrt(jnp.mean(x * x, axis=-1, keepdims=True) + EPS) * g


def _split2_dot(x, m):
    hi = x.astype(BF16)
    lo = (x - hi.astype(F32)).astype(BF16)
    return _dot(hi, m) + _dot(lo, m)


def _params(n_grid_dims):
    return pltpu.CompilerParams(
        dimension_semantics=("arbitrary",) * n_grid_dims,
        vmem_limit_bytes=VMEM_LIMIT_BYTES)


def _resident(arr):
    nd = arr.ndim
    return pl.BlockSpec(arr.shape, lambda *_: (0,) * nd)


def _proj_in_kernel(x_ref, g_ref, w_ref, gq_ref, gk_ref, bf_ref, grp_ref,
                    qf_ref, kf_ref, vf_ref, qs_ref, ks_ref, vs_ref, lf_ref, *, width):
    h = _rms(x_ref[...], g_ref[...]).astype(BF16)
    scale = HEAD_DIM ** -0.5

    def seg(i, w=width):
        return _dot(h, w_ref[:, i * width:i * width + w])

    def head_norm(p, g_row, out_ref, mul):
        sq = p * p
        for c in range(width // LANES):
            sl = slice(c * LANES, (c + 1) * LANES)
            ss = _split2_dot(sq[:, sl], grp_ref[...])
            y = p[:, sl] * lax.rsqrt(ss * (1.0 / HEAD_DIM) + EPS) * g_row[:, sl]
            out_ref[:, sl] = (y * mul).astype(out_ref.dtype)

    head_norm(seg(0), gq_ref[...], qf_ref, scale)
    head_norm(seg(1), gk_ref[...], kf_ref, 1.0)
    vf_ref[...] = seg(2)
    qs_ref[...] = (seg(3) * scale).astype(BF16)
    ks_ref[...] = seg(4)
    vs_ref[...] = seg(5)
    lf_ref[...] = jax.nn.log_sigmoid(seg(6, LANES) + bf_ref[...])


def _proj_in(x2d, P):
    n, d = x2d.shape
    width = P["gq"].shape[1]
    tm = min(256, n)
    row = lambda w: pl.BlockSpec((tm, w), lambda i: (i, 0))
    ins = [x2d, P["g_mix"], P["w_in"], P["gq"], P["gk"], P["b_forget"], P["grp"]]
    sds = jax.ShapeDtypeStruct
    return pl.pallas_call(
        functools.partial(_proj_in_kernel, width=width),
        grid=(n // tm,),
        in_specs=[row(d)] + [_resident(a) for a in ins[1:]],
        out_specs=[row(width)] * 6 + [row(LANES)],
        out_shape=[sds((n, width), BF16), sds((n, width), F32), sds((n, width), F32),
                   sds((n, width), BF16), sds((n, width), F32), sds((n, width), F32),
                   sds((n, LANES), F32)],
        compiler_params=_params(1),
        name="proj_in",
    )(*ins)


def _cumsum_kernel(lf_ref, tri_ref, f_ref, ft_ref, carry_ref):
    @pl.when(pl.program_id(1) == 0)
    def _():
        carry_ref[...] = jnp.zeros_like(carry_ref)

    x = lf_ref[0]
    tri = tri_ref[...]
    hi = x.astype(BF16)
    r1 = x - hi.astype(F32)
    mid = r1.astype(BF16)
    lo = (r1 - mid.astype(F32)).astype(BF16)
    f = _dot(tri, hi) + _dot(tri, mid) + _dot(tri, lo) + carry_ref[...]
    f_ref[0] = f
    ft_ref[0] = f.T[0:8, :]
    tb = x.shape[0]
    carry_ref[...] = f[tb - 1:tb, :]


def _cumsum_logf(lf_slab, tb):
    b, t, _ = lf_slab.shape
    r = lax.broadcasted_iota(jnp.int32, (tb, tb), 0)
    c = lax.broadcasted_iota(jnp.int32, (tb, tb), 1)
    tri = (c <= r).astype(BF16)
    sds = jax.ShapeDtypeStruct
    return pl.pallas_call(
        _cumsum_kernel,
        grid=(b, t // tb),
        in_specs=[pl.BlockSpec((1, tb, LANES), lambda i, j: (i, j, 0)), _resident(tri)],
        out_specs=[pl.BlockSpec((1, tb, LANES), lambda i, j: (i, j, 0)),
                   pl.BlockSpec((1, 8, tb), lambda i, j: (i, 0, j))],
        out_shape=[sds((b, t, LANES), F32), sds((b, 8, t), F32)],
        scratch_shapes=[pltpu.VMEM((1, LANES), F32)],
        compiler_params=_params(2),
        name="cumsum_logf",
    )(lf_slab, tri)


def _stage_kv(k_ref, v_ref, kb_scr, v0_scr, v1_scr, is0):
    kb_scr[...] = k_ref[0].astype(BF16)
    v = v_ref[0]
    v0_scr[...] = jnp.where(is0, v, 0.0).astype(BF16)
    v1_scr[...] = jnp.where(is0, 0.0, v).astype(BF16)


def _block_range(qi, tq, tk, off):
    q_start = off + qi * tq
    n_blocks = (q_start + tq + tk - 1) // tk
    n_full = (q_start + 1) // tk
    return q_start, n_blocks, n_full


def _fox_kernel(q_ref, k_ref, v_ref, fq_ref, ft_ref, o_ref,
                kb_scr, v0_scr, v1_scr, m_scr, l_scr, acc_scr, *, tq, tk, off):
    hp = pl.program_id(1)
    qi = pl.program_id(2)
    is0 = lax.broadcasted_iota(jnp.int32, (1, LANES), 1) < HEAD_DIM

    @pl.when(qi == 0)
    def _():
        _stage_kv(k_ref, v_ref, kb_scr, v0_scr, v1_scr, is0)

    q = q_ref[0]
    zero = jnp.zeros_like(q)
    qs = (jnp.where(is0, q, zero), jnp.where(is0, zero, q))
    fq_blk = fq_ref[0]
    lane = lax.broadcasted_iota(jnp.int32, fq_blk.shape, 1)
    fqs = tuple(jnp.sum(jnp.where(lane == 2 * hp + j, fq_blk, 0.0), axis=1, keepdims=True)
                for j in range(2))
    v_scrs = (v0_scr, v1_scr)

    m_scr[...] = jnp.full_like(m_scr, NEG)
    l_scr[...] = jnp.zeros_like(l_scr)
    acc_scr[...] = jnp.zeros_like(acc_scr)
    q_start, n_blocks, n_full = _block_range(qi, tq, tk, off)

    def step(kb, masked):
        ks = pl.multiple_of(kb * tk, tk)
        kblk = kb_scr[pl.ds(ks, tk), :]
        if masked:
            qpos = q_start + lax.broadcasted_iota(jnp.int32, (tq, tk), 0)
            kpos = ks + lax.broadcasted_iota(jnp.int32, (tq, tk), 1)
            visible = kpos <= qpos
        alphas, pvs = [], []
        for j in range(2):
            fk = ft_ref[0, pl.ds(2 * hp + j, 1), pl.ds(ks, tk)]
            u = _dot_nt(qs[j], kblk) - fk
            if masked:
                u = jnp.where(visible, u, NEG)
            m_prev = m_scr[j]
            m_new = jnp.maximum(m_prev, jnp.max(u, axis=1, keepdims=True) + fqs[j])
            alpha = jnp.exp(m_prev - m_new)
            p = jnp.exp(u - (m_new - fqs[j]))
            l_scr[j] = alpha * l_scr[j] + jnp.sum(p, axis=1, keepdims=True)
            m_scr[j] = m_new
            alphas.append(alpha)
            pvs.append(_dot(p.astype(BF16), v_scrs[j][pl.ds(ks, tk), :]))
        acc_scr[...] = acc_scr[...] * jnp.where(is0, alphas[0], alphas[1]) + pvs[0] + pvs[1]

    def masked_body(i, _):
        step(n_blocks - 1 - i, True)
        return 0

    def full_body(i, _):
        step(n_full - 1 - i, False)
        return 0

    lax.fori_loop(0, n_blocks - n_full, masked_body, 0)
    lax.fori_loop(0, n_full, full_body, 0)
    o_ref[0] = acc_scr[...] / jnp.where(is0, l_scr[0], l_scr[1])


def _sb_kernel(q_ref, k_ref, v_ref, tri_ref, o_ref,
               kb_scr, v0_scr, v1_scr, c_scr, acc_scr, *, tq, tk, off):
    qi = pl.program_id(2)
    is0 = lax.broadcasted_iota(jnp.int32, (1, LANES), 1) < HEAD_DIM

    @pl.when(qi == 0)
    def _():
        _stage_kv(k_ref, v_ref, kb_scr, v0_scr, v1_scr, is0)

    q = q_ref[0]
    zero = jnp.zeros_like(q)
    qs = (jnp.where(is0, q, zero), jnp.where(is0, zero, q))
    v_scrs = (v0_scr, v1_scr)
    c_scr[...] = jnp.zeros_like(c_scr)
    acc_scr[...] = jnp.zeros_like(acc_scr)
    q_start, n_blocks, n_full = _block_range(qi, tq, tk, off)

    def step(kb, masked):
        ks = pl.multiple_of(kb * tk, tk)
        kblk = kb_scr[pl.ds(ks, tk), :]
        if masked:
            qpos = q_start + lax.broadcasted_iota(jnp.int32, (tq, tk), 0)
            kpos = ks + lax.broadcasted_iota(jnp.int32, (tq, tk), 1)
            strict = kpos < qpos
        total = None
        for j in range(2):
            z = _dot_nt(qs[j], kblk)
            log_1m = -(jnp.maximum(z, 0.0) + jnp.log(1.0 + jnp.exp(-jnp.abs(z))))
            if masked:
                log_1m = jnp.where(strict, log_1m, 0.0)
            rest = _dot(log_1m.astype(BF16), tri_ref[...]) + c_scr[j]
            arg = z + log_1m + rest
            if masked:
                arg = jnp.where(strict, arg, NEG)
            a = jnp.exp(arg)
            c_scr[j] = c_scr[j] + jnp.sum(log_1m, axis=1, keepdims=True)
            pv = _dot(a.astype(BF16), v_scrs[j][pl.ds(ks, tk), :])
            total = pv if total is None else total + pv
        acc_scr[...] = acc_scr[...] + total

    def masked_body(i, _):
        step(n_blocks - 1 - i, True)
        return 0

    def full_body(i, _):
        step(n_full - 1 - i, False)
        return 0

    lax.fori_loop(0, n_blocks - n_full, masked_body, 0)
    lax.fori_loop(0, n_full, full_body, 0)
    o_ref[0] = acc_scr[...]


def _attn_tiles(tq_total):
    tq = 256 if tq_total % 256 == 0 else tq_total
    return tq, 256


def _attn_specs(b, tq_total, tk_total, width, tq):
    pairs = width // LANES
    grid = (b, pairs, tq_total // tq)
    q_spec = pl.BlockSpec((1, tq, LANES), lambda i, p, j: (i, j, p))
    kv_spec = pl.BlockSpec((1, tk_total, LANES), lambda i, p, j: (i, 0, p))
    kv_scratch = [pltpu.VMEM((tk_total, LANES), BF16)] * 3
    return grid, q_spec, kv_spec, kv_scratch


def _fox_attention(q, k, v, f, ft, off):
    b, tq_total, width = q.shape
    tk_total = k.shape[1]
    tq, tk = _attn_tiles(tq_total)
    assert tk_total % tk == 0 and off % tq == 0 and off + tq_total <= tk_total
    grid, q_spec, kv_spec, kv_scratch = _attn_specs(b, tq_total, tk_total, width, tq)
    return pl.pallas_call(
        functools.partial(_fox_kernel, tq=tq, tk=tk, off=off),
        grid=grid,
        in_specs=[q_spec, kv_spec, kv_spec,
                  pl.BlockSpec((1, tq, LANES), lambda i, p, j: (i, off // tq + j, 0)),
                  pl.BlockSpec((1, 8, tk_total), lambda i, p, j: (i, 0, 0))],
        out_specs=q_spec,
        out_shape=jax.ShapeDtypeStruct((b, tq_total, width), F32),
        scratch_shapes=kv_scratch + [pltpu.VMEM((2, tq, 1), F32), pltpu.VMEM((2, tq, 1), F32),
                                     pltpu.VMEM((tq, LANES), F32)],
        compiler_params=_params(3),
        name="fox_attention",
    )(q, k, v, f, ft)


def _sb_attention(q, k, v, off):
    b, tq_total, width = q.shape
    tk_total = k.shape[1]
    tq, tk = _attn_tiles(tq_total)
    assert tk_total % tk == 0 and off % tq == 0 and off + tq_total <= tk_total
    grid, q_spec, kv_spec, kv_scratch = _attn_specs(b, tq_total, tk_total, width, tq)
    r = lax.broadcasted_iota(jnp.int32, (tk, tk), 0)
    c = lax.broadcasted_iota(jnp.int32, (tk, tk), 1)
    tri = (r > c).astype(BF16)
    return pl.pallas_call(
        functools.partial(_sb_kernel, tq=tq, tk=tk, off=off),
        grid=grid,
        in_specs=[q_spec, kv_spec, kv_spec, _resident(tri)],
        out_specs=q_spec,
        out_shape=jax.ShapeDtypeStruct((b, tq_total, width), F32),
        scratch_shapes=kv_scratch + [pltpu.VMEM((2, tq, 1), F32), pltpu.VMEM((tq, LANES), F32)],
        compiler_params=_params(3),
        name="sb_attention",
    )(q, k, v, tri)


def _out_proj_kernel(of_ref, os_ref, x_ref, gf_ref, gs_ref, w_ref, y_ref):
    a = _rms(of_ref[...], gf_ref[...]).astype(BF16)
    b = _rms(os_ref[...], gs_ref[...]).astype(BF16)
    nf = a.shape[1]
    y_ref[...] = x_ref[...] + _dot(a, w_ref[0:nf, :]) + _dot(b, w_ref[nf:, :])


def _out_proj(o_f, o_s, x2d, P):
    n, d = x2d.shape
    tm = min(512, n)
    row = lambda w: pl.BlockSpec((tm, w), lambda i: (i, 0))
    ins = [o_f, o_s, x2d, P["g_out_fox"], P["g_out_sb"], P["w_out"]]
    return pl.pallas_call(
        _out_proj_kernel,
        grid=(n // tm,),
        in_specs=[row(o_f.shape[1]), row(o_s.shape[1]), row(d)] + [_resident(a) for a in ins[3:]],
        out_specs=row(d),
        out_shape=jax.ShapeDtypeStruct((n, d), F32),
        compiler_params=_params(1),
        name="out_proj",
    )(*ins)


def _cross_kernel(x_ref, mk_ref, mv_ref, g_ref, wq_ref, gq_ref, wo_ref, y_ref):
    x = x_ref[0]
    h = _rms(x, g_ref[...]).astype(BF16)
    hd = wq_ref.shape[1] // MEM_HEADS
    scale = hd ** -0.5
    out = x
    for i in range(MEM_HEADS):
        sl = slice(i * hd, (i + 1) * hd)
        q = (_rms(_dot(h, wq_ref[:, sl]), gq_ref[...]) * scale).astype(BF16)
        s = _dot_nt(q, mk_ref[0, :, sl].astype(BF16))
        e = jnp.exp(s - jnp.max(s, axis=1, keepdims=True))
        o = _dot(e.astype(BF16), mv_ref[0, :, sl].astype(BF16)) / jnp.sum(e, axis=1, keepdims=True)
        out = out + _dot(o.astype(BF16), wo_ref[sl, :])
    y_ref[0] = out


def _cross_attention(x, mem_k, mem_v, P):
    b, t, d = x.shape
    tm = min(512, t)
    row = pl.BlockSpec((1, tm, d), lambda i, j: (i, j, 0))
    mem = pl.BlockSpec((1,) + mem_k.shape[1:], lambda i, j: (i, 0, 0))
    ins = [x, mem_k, mem_v, P["g_cross"], P["w_mq"], P["g_mq"], P["w_mo"]]
    return pl.pallas_call(
        _cross_kernel,
        grid=(b, t // tm),
        in_specs=[row, mem, mem] + [_resident(a) for a in ins[3:]],
        out_specs=row,
        out_shape=jax.ShapeDtypeStruct((b, t, d), F32),
        compiler_params=_params(2),
        name="cross_attention",
    )(*ins)


def _ffn_kernel(x_ref, g_ref, w1_ref, w2_ref, y_ref, *, chunk):
    x = x_ref[...]
    h = _rms(x, g_ref[...]).astype(BF16)
    acc = x
    for c in range(w1_ref.shape[1] // chunk):
        sl = slice(c * chunk, (c + 1) * chunk)
        u = jnp.square(jnp.maximum(_dot(h, w1_ref[:, sl]), 0.0)).astype(BF16)
        acc = acc + _dot(u, w2_ref[sl, :])
    y_ref[...] = acc


def _ffn(x2d, P):
    n, d = x2d.shape
    tm = min(512, n)
    row = pl.BlockSpec((tm, d), lambda i: (i, 0))
    once = lambda a: pl.BlockSpec(a.shape, lambda i: (0, 0), pipeline_mode=pl.Buffered(1))
    return pl.pallas_call(
        functools.partial(_ffn_kernel, chunk=1024),
        grid=(n // tm,),
        in_specs=[row, _resident(P["g_ffn"]), once(P["w_ff1"]), once(P["w_ff2"])],
        out_specs=row,
        out_shape=jax.ShapeDtypeStruct((n, d), F32),
        compiler_params=_params(1),
        name="ffn",
    )(x2d, P["g_ffn"], P["w_ff1"], P["w_ff2"])


def _mem_kv_kernel(m_ref, g_ref, wk_ref, wv_ref, gk_ref, k_ref, v_ref):
    h = _rms(m_ref[...], g_ref[...]).astype(BF16)
    hd = wk_ref.shape[1] // MEM_HEADS
    for i in range(MEM_HEADS):
        sl = slice(i * hd, (i + 1) * hd)
        k_ref[:, sl] = _rms(_dot(h, wk_ref[:, sl]), gk_ref[...])
    v_ref[...] = _dot(h, wv_ref[...])


def _memory_kv(mem2d, P):
    n, d = mem2d.shape
    w = P["w_mk"].shape[1]
    tm = min(256, n)
    row = lambda c: pl.BlockSpec((tm, c), lambda i: (i, 0))
    ins = [mem2d, P["g_mem"], P["w_mk"], P["w_mv"], P["g_mk"]]
    return pl.pallas_call(
        _mem_kv_kernel,
        grid=(n // tm,),
        in_specs=[row(d)] + [_resident(a) for a in ins[1:]],
        out_specs=[row(w), row(w)],
        out_shape=[jax.ShapeDtypeStruct((n, w), F32)] * 2,
        compiler_params=_params(1),
        name="memory_kv",
    )(*ins)


def _pad_rows(a, t):
    return jnp.pad(a, ((0, 0), (0, t - a.shape[1]), (0, 0)))


def _layer(x, mem_k, mem_v, past, P):
    b, t, d = x.shape
    x2d = x.reshape(b * t, d)
    q_f, k_f, v_f, q_s, k_s, v_s, lf = _proj_in(x2d, P)
    width = k_f.shape[1]
    new = [a.reshape(b, t, -1) for a in (k_f, v_f, lf, k_s, v_s)]
    if past is None:
        kf_all, vf_all, lf_all, ks_all, vs_all = new
        off = 0
    else:
        off = past[0].shape[1]
        tk_total = -(-(off + t) // 256) * 256
        past = [a.reshape(b, off, -1) for a in past]
        past[2] = jnp.pad(past[2].astype(F32), ((0, 0), (0, 0), (0, LANES - past[2].shape[2])))
        kf_all, vf_all, lf_all, ks_all, vs_all = [
            _pad_rows(jnp.concatenate([p, n], axis=1), tk_total) for p, n in zip(past, new)]
    tk_total = kf_all.shape[1]
    f, ft = _cumsum_logf(lf_all, 512 if tk_total % 512 == 0 else 256)
    o_f = _fox_attention(q_f.reshape(b, t, width), kf_all, vf_all, f, ft, off)
    o_s = _sb_attention(q_s.reshape(b, t, width), ks_all, vs_all, off)
    y = _out_proj(o_f.reshape(b * t, width), o_s.reshape(b * t, width), x2d, P)
    y = _cross_attention(y.reshape(b, t, d), mem_k, mem_v, P)
    y = _ffn(y.reshape(b * t, d), P)
    return y.reshape(b, t, d), (k_f, v_f, lf[:, :P["n_fox_heads"]], k_s, v_s)


def _layer_params(l, g_mix, w_in, b_forget, g_fox_q, g_fox_k, g_out_fox, g_out_sb, w_out,
                  g_cross, g_mem, w_mq, w_mk, w_mv, g_mq, g_mk, w_mo, g_ffn, w_ff1, w_ff2):
    fox_w = g_out_fox.shape[1]
    sb_w = g_out_sb.shape[1]
    assert fox_w == sb_w
    n_heads = b_forget.shape[1]
    w = w_in[l]
    o_sb = 3 * fox_w + n_heads
    w_re = jnp.concatenate(
        [w[:, :3 * fox_w], w[:, o_sb:o_sb + 3 * sb_w],
         jnp.pad(w[:, 3 * fox_w:o_sb], ((0, 0), (0, LANES - n_heads)))], axis=1).astype(BF16)
    row = lambda a: a[l].reshape(1, -1).astype(F32)
    lane = jnp.arange(LANES) // HEAD_DIM
    return dict(
        n_fox_heads=n_heads,
        g_mix=row(g_mix), w_in=w_re,
        gq=jnp.tile(row(g_fox_q), (1, fox_w // HEAD_DIM)),
        gk=jnp.tile(row(g_fox_k), (1, fox_w // HEAD_DIM)),
        b_forget=jnp.pad(row(b_forget), ((0, 0), (0, LANES - n_heads))),
        grp=(lane[:, None] == lane[None, :]).astype(BF16),
        g_out_fox=row(g_out_fox), g_out_sb=row(g_out_sb), w_out=w_out[l].astype(BF16),
        g_cross=row(g_cross), w_mq=w_mq[l].astype(BF16), g_mq=row(g_mq), w_mo=w_mo[l].astype(BF16),
        g_mem=row(g_mem), w_mk=w_mk[l].astype(BF16), w_mv=w_mv[l].astype(BF16), g_mk=row(g_mk),
        g_ffn=row(g_ffn), w_ff1=w_ff1[l].astype(BF16), w_ff2=w_ff2[l].astype(BF16))


def kernel(x_prompt, x_sample, mem_prompt, cache_fox_k, cache_fox_v, cache_fox_logf, cache_sb_k, cache_sb_v, cache_mem_k, cache_mem_v, g_mix, w_in, b_forget, g_fox_q, g_fox_k, g_out_fox, g_out_sb, w_out, g_cross, g_mem, w_mq, w_mk, w_mv, g_mq, g_mk, w_mo, g_ffn, w_ff1, w_ff2):
    depth = g_mix.shape[0]
    bp, tp, d = x_prompt.shape
    bs, ts, _ = x_sample.shape
    n_mem = mem_prompt.shape[1]
    xp, xs = x_prompt, x_sample
    mem2d = mem_prompt.reshape(bp * n_mem, d)
    outs_p = [[] for _ in range(7)]
    outs_s = [[] for _ in range(5)]
    for l in range(depth):
        P = _layer_params(l, g_mix, w_in, b_forget, g_fox_q, g_fox_k, g_out_fox, g_out_sb, w_out,
                          g_cross, g_mem, w_mq, w_mk, w_mv, g_mq, g_mk, w_mo, g_ffn, w_ff1, w_ff2)
        n_heads = P["n_fox_heads"]
        mk, mv = _memory_kv(mem2d, P)
        mk = mk.reshape(bp, n_mem, -1)
        mv = mv.reshape(bp, n_mem, -1)
        xp, new_p = _layer(xp, mk, mv, None, P)
        for dst, a in zip(outs_p, new_p + (mk, mv)):
            dst.append(a)
        past = (cache_fox_k[l], cache_fox_v[l], cache_fox_logf[l], cache_sb_k[l], cache_sb_v[l])
        xs, new_s = _layer(xs, cache_mem_k[l].reshape(bs, n_mem, -1), cache_mem_v[l].reshape(bs, n_mem, -1),
                           past, P)
        for dst, a in zip(outs_s, new_s):
            dst.append(a)

    def heads(stack, b, t, hd):
        return jnp.stack(stack).reshape(depth, b, t, -1, hd)

    p_fk, p_fv, p_lf, p_sk, p_sv, p_mk, p_mv = outs_p
    s_fk, s_fv, s_lf, s_sk, s_sv = outs_s
    mem_hd = p_mk[0].shape[-1] // MEM_HEADS
    return (xp, xs,
            heads(p_fk, bp, tp, HEAD_DIM), heads(p_fv, bp, tp, HEAD_DIM),
            jnp.stack(p_lf).reshape(depth, bp, tp, n_heads),
            heads(p_sk, bp, tp, HEAD_DIM), heads(p_sv, bp, tp, HEAD_DIM),
            heads(p_mk, bp, n_mem, mem_hd), heads(p_mv, bp, n_mem, mem_hd),
            heads(s_fk, bs, ts, HEAD_DIM), heads(s_fv, bs, ts, HEAD_DIM),
            jnp.stack(s_lf).reshape(depth, bs, ts, n_heads),
            heads(s_sk, bs, ts, HEAD_DIM), heads(s_sv, bs, ts, HEAD_DIM))
```

```python
import functools

import numpy as np
import jax
import jax.numpy as jnp
from jax import lax
from jax.experimental import pallas as pl
from jax.experimental.pallas import tpu as pltpu

EPS = 1e-6
NEG = -1e30
HEAD_DIM = 64
MEM_HEADS = 4
LANES = 128
VMEM_LIMIT_BYTES = 56 * 2**20
EXP_ZERO_BELOW = -106.0
F32 = jnp.float32
BF16 = jnp.bfloat16


def _dot(a, b):
    return jnp.dot(a, b, preferred_element_type=F32)


def _dot_nt(a, b):
    return lax.dot_general(a, b, (((1,), (1,)), ((), ())), preferred_element_type=F32)


def _rms(x, g):
    return x * lax.rsqrt(jnp.mean(x * x, axis=-1, keepdims=True) + EPS) * g


def _split3(x):
    hi = x.astype(BF16)
    r = x - hi.astype(F32)
    mid = r.astype(BF16)
    lo = (r - mid.astype(F32)).astype(BF16)
    return hi, mid, lo


def _split2_dot(x, m):
    hi = x.astype(BF16)
    lo = (x - hi.astype(F32)).astype(BF16)
    return _dot(hi, m) + _dot(lo, m)


def _params(n_grid_dims):
    return pltpu.CompilerParams(
        dimension_semantics=("arbitrary",) * n_grid_dims,
        vmem_limit_bytes=VMEM_LIMIT_BYTES)


def _resident(arr):
    nd = arr.ndim
    return pl.BlockSpec(arr.shape, lambda *_: (0,) * nd)


_SMEM = pl.BlockSpec(memory_space=pltpu.SMEM)


def _proj_in_kernel(x_ref, g_ref, w_ref, gq_ref, gk_ref, bf_ref, grp_ref,
                    qf_ref, kf_ref, vf_ref, qs_ref, ks_ref, vs_ref, lf_ref, *, width):
    h = _rms(x_ref[...], g_ref[...]).astype(BF16)
    scale = HEAD_DIM ** -0.5

    def seg(i, w=width):
        return _dot(h, w_ref[:, i * width:i * width + w])

    def head_norm(p, g_row, out_ref, mul):
        sq = p * p
        for c in range(width // LANES):
            sl = slice(c * LANES, (c + 1) * LANES)
            ss = _split2_dot(sq[:, sl], grp_ref[...])
            y = p[:, sl] * lax.rsqrt(ss * (1.0 / HEAD_DIM) + EPS) * g_row[:, sl]
            out_ref[:, sl] = (y * mul).astype(out_ref.dtype)

    head_norm(seg(0), gq_ref[...], qf_ref, scale)
    head_norm(seg(1), gk_ref[...], kf_ref, 1.0)
    vf_ref[...] = seg(2)
    qs_ref[...] = (seg(3) * scale).astype(BF16)
    ks_ref[...] = seg(4)
    vs_ref[...] = seg(5)
    lf_ref[...] = jax.nn.log_sigmoid(seg(6, LANES) + bf_ref[...])


def _proj_in(x2d, P):
    n, d = x2d.shape
    width = P["gq"].shape[1]
    tm = min(256, n)
    row = lambda w: pl.BlockSpec((tm, w), lambda i: (i, 0))
    ins = [x2d, P["g_mix"], P["w_in"], P["gq"], P["gk"], P["b_forget"], P["grp"]]
    sds = jax.ShapeDtypeStruct
    return pl.pallas_call(
        functools.partial(_proj_in_kernel, width=width),
        grid=(n // tm,),
        in_specs=[row(d)] + [_resident(a) for a in ins[1:]],
        out_specs=[row(width)] * 6 + [row(LANES)],
        out_shape=[sds((n, width), BF16), sds((n, width), F32), sds((n, width), F32),
                   sds((n, width), BF16), sds((n, width), F32), sds((n, width), F32),
                   sds((n, LANES), F32)],
        compiler_params=_params(1),
        name="proj_in",
    )(*ins)


def _gate_lane_constants(n_pairs):
    pk = np.zeros((n_pairs, 3, LANES, LANES), np.float32)
    pq = np.zeros((n_pairs, 3, LANES, LANES), np.float32)
    kc = np.zeros((n_pairs, 1, LANES), np.float32)
    qc = np.zeros((n_pairs, 1, LANES), np.float32)
    for p in range(n_pairs):
        for i in range(3):
            pk[p, i, 2 * p, HEAD_DIM + i] = -1.0
            pk[p, i, 2 * p + 1, i] = -1.0
            pq[p, i, 2 * p, HEAD_DIM + 3 + i] = 1.0
            pq[p, i, 2 * p + 1, 3 + i] = 1.0
        kc[p, 0, HEAD_DIM + 3:HEAD_DIM + 6] = 1.0
        kc[p, 0, 3:6] = 1.0
        qc[p, 0, HEAD_DIM:HEAD_DIM + 3] = 1.0
        qc[p, 0, 0:3] = 1.0
    return jnp.asarray(pk, BF16), jnp.asarray(pq, BF16), jnp.asarray(kc), jnp.asarray(qc)


def _cumsum_kernel(lf_ref, tri_ref, pk_ref, pq_ref, kc_ref, qc_ref,
                   f_ref, ft_ref, kt_ref, qt_ref, carry_ref):
    @pl.when(pl.program_id(1) == 0)
    def _():
        carry_ref[...] = jnp.zeros_like(carry_ref)

    tri = tri_ref[...]
    f = sum(_dot(tri, part) for part in _split3(lf_ref[0])) + carry_ref[...]
    f_ref[0] = f
    ft_ref[0] = f.T[0:8, :]
    tb = f.shape[0]
    carry_ref[...] = f[tb - 1:tb, :]
    parts = _split3(f)
    for p in range(kt_ref.shape[1]):
        kt = sum(_dot(parts[i], pk_ref[p, i]) for i in range(3)) + kc_ref[p]
        qt = sum(_dot(parts[i], pq_ref[p, i]) for i in range(3)) + qc_ref[p]
        kt_ref[0, p] = kt.astype(BF16)
        qt_ref[0, p] = qt.T.astype(BF16)


def _cumsum_logf(lf_slab, tb, n_pairs):
    b, t, _ = lf_slab.shape
    r = lax.broadcasted_iota(jnp.int32, (tb, tb), 0)
    c = lax.broadcasted_iota(jnp.int32, (tb, tb), 1)
    tri = (c <= r).astype(BF16)
    consts = _gate_lane_constants(n_pairs)
    sds = jax.ShapeDtypeStruct
    return pl.pallas_call(
        _cumsum_kernel,
        grid=(b, t // tb),
        in_specs=[pl.BlockSpec((1, tb, LANES), lambda i, j: (i, j, 0)), _resident(tri)]
                 + [_resident(a) for a in consts],
        out_specs=[pl.BlockSpec((1, tb, LANES), lambda i, j: (i, j, 0)),
                   pl.BlockSpec((1, 8, tb), lambda i, j: (i, 0, j)),
                   pl.BlockSpec((1, n_pairs, tb, LANES), lambda i, j: (i, 0, j, 0)),
                   pl.BlockSpec((1, n_pairs, LANES, tb), lambda i, j: (i, 0, 0, j))],
        out_shape=[sds((b, t, LANES), F32), sds((b, 8, t), F32),
                   sds((b, n_pairs, t, LANES), BF16), sds((b, n_pairs, LANES, t), BF16)],
        scratch_shapes=[pltpu.VMEM((1, LANES), F32)],
        compiler_params=_params(2),
        name="cumsum_logf",
    )(lf_slab, tri, *consts)


def _positions(q_start, ks, tq, tk):
    kpos = ks + lax.broadcasted_iota(jnp.int32, (tk, tq), 0)
    qpos = q_start + lax.broadcasted_iota(jnp.int32, (tk, tq), 1)
    return kpos, qpos


def _walk_key_blocks(step, q_start, tk):
    kb_diag = q_start // tk
    go = step(kb_diag, True)

    def cond(carry):
        kb, go = carry
        return jnp.logical_and(kb >= 0, go)

    def body(carry):
        kb, _ = carry
        return kb - 1, step(kb, False)

    lax.while_loop(cond, body, (kb_diag - 1, go))


def _fox_kernel(bound_ref, fend_ref, q_ref, k_ref, v_ref, kt_ref, qt_ref, ft_ref, o_ref,
                k_scr, vt_scr, m_scr, l_scr, acc_scr, *, tq, tk, off, n_heads):
    b = pl.program_id(0)
    group = pl.program_id(1)
    qi = pl.program_id(2)
    n_kb = k_ref.shape[1] // tk
    n_pairs = kt_ref.shape[1]
    lane_lo = lax.broadcasted_iota(jnp.int32, (1, LANES), 1) < HEAD_DIM
    row_lo = lax.broadcasted_iota(jnp.int32, (LANES, 1), 0) < HEAD_DIM

    @pl.when(qi == 0)
    def _():
        for c in range(n_kb):
            rows = slice(c * tk, (c + 1) * tk)
            for p in range(n_pairs):
                lanes = slice(p * LANES, (p + 1) * LANES)
                k = k_ref[0, rows, lanes].astype(BF16)
                t = kt_ref[0, p, rows, :]
                k_scr[2 * p, rows, :] = jnp.where(lane_lo, k, t)
                k_scr[2 * p + 1, rows, :] = jnp.where(lane_lo, t, k)
                vt_scr[lanes, rows] = v_ref[0, rows, lanes].astype(F32).T.astype(BF16)

    qs, fqs = [], []
    for p in range(n_pairs):
        q_t = q_ref[0, :, p * LANES:(p + 1) * LANES].astype(F32).T.astype(BF16)
        t = qt_ref[0, p]
        qs += [jnp.where(row_lo, q_t, t), jnp.where(row_lo, t, q_t)]
    first_head = 2 * n_pairs * group
    for h in range(2 * n_pairs):
        fqs.append(ft_ref[0, pl.ds(first_head + h, 1), :])

    m_scr[...] = jnp.full_like(m_scr, NEG)
    l_scr[...] = jnp.zeros_like(l_scr)
    acc_scr[...] = jnp.zeros_like(acc_scr)
    q_start = off + qi * tq

    def step(kb, masked):
        ks = pl.multiple_of(kb * tk, tk)
        if masked:
            kpos, qpos = _positions(q_start, ks, tq, tk)
            visible = kpos <= qpos
        scores = [_dot(k_scr[h, pl.ds(ks, tk), :], qs[h]) for h in range(2 * n_pairs)]
        go = False
        for h in range(2 * n_pairs):
            s = scores[h]
            if masked:
                s = jnp.where(visible, s, NEG)
            m_prev = m_scr[h]
            m_new = jnp.maximum(m_prev, jnp.max(s, axis=0, keepdims=True))
            alpha = jnp.exp(m_prev - m_new)
            p = jnp.exp(s - m_new)
            l_scr[h] = alpha * l_scr[h] + jnp.sum(p, axis=0, keepdims=True)
            m_scr[h] = m_new
            vt = vt_scr[h * HEAD_DIM:(h + 1) * HEAD_DIM, pl.ds(ks, tk)]
            acc_scr[h] = acc_scr[h] * alpha + _dot(vt, p.astype(BF16))
            f_end = fend_ref[(b * n_heads + first_head + h) * n_kb + jnp.maximum(kb - 1, 0)]
            reach = jnp.max(fqs[h] - m_new) + bound_ref[0] - f_end
            go = jnp.logical_or(go, reach >= EXP_ZERO_BELOW)
        return go

    _walk_key_blocks(step, q_start, tk)
    o_t = jnp.concatenate([acc_scr[h] / l_scr[h] for h in range(2 * n_pairs)], axis=0)
    o_ref[0] = o_t.T


def _sb_kernel(q_ref, k_ref, v_ref, tri_ref, o_ref,
               kb_scr, vt_scr, c_scr, acc_scr, *, tq, tk, off):
    qi = pl.program_id(2)
    n_kb = k_ref.shape[1] // tk
    n_pairs = k_ref.shape[2] // LANES
    row_lo = lax.broadcasted_iota(jnp.int32, (LANES, 1), 0) < HEAD_DIM

    @pl.when(qi == 0)
    def _():
        for c in range(n_kb):
            rows = slice(c * tk, (c + 1) * tk)
            kb_scr[rows, :] = k_ref[0, rows, :].astype(BF16)
            for p in range(n_pairs):
                lanes = slice(p * LANES, (p + 1) * LANES)
                vt_scr[lanes, rows] = v_ref[0, rows, lanes].astype(F32).T.astype(BF16)

    qs = []
    for p in range(n_pairs):
        q_t = q_ref[0, :, p * LANES:(p + 1) * LANES].astype(F32).T.astype(BF16)
        zero = jnp.zeros_like(q_t)
        qs += [jnp.where(row_lo, q_t, zero), jnp.where(row_lo, zero, q_t)]
    c_scr[...] = jnp.zeros_like(c_scr)
    acc_scr[...] = jnp.zeros_like(acc_scr)
    q_start = off + qi * tq

    def step(kb, masked):
        ks = pl.multiple_of(kb * tk, tk)
        if masked:
            kpos, qpos = _positions(q_start, ks, tq, tk)
            strict = kpos < qpos
        zs = [_dot(kb_scr[pl.ds(ks, tk), (h // 2) * LANES:(h // 2 + 1) * LANES], qs[h])
              for h in range(2 * n_pairs)]
        reach = None
        for h in range(2 * n_pairs):
            z = zs[h]
            log_1m = -(jnp.maximum(z, 0.0) + jnp.log(1.0 + jnp.exp(-jnp.abs(z))))
            if masked:
                log_1m = jnp.where(strict, log_1m, 0.0)
            rest = _dot(tri_ref[...], log_1m.astype(BF16)) + c_scr[h]
            arg = z + log_1m + rest
            if masked:
                arg = jnp.where(strict, arg, NEG)
            a = jnp.exp(arg)
            c_new = c_scr[h] + jnp.sum(log_1m, axis=0, keepdims=True)
            c_scr[h] = c_new
            vt = vt_scr[h * HEAD_DIM:(h + 1) * HEAD_DIM, pl.ds(ks, tk)]
            acc_scr[h] = acc_scr[h] + _dot(vt, a.astype(BF16))
            reach = c_new if reach is None else jnp.maximum(reach, c_new)
        return jnp.max(reach) >= EXP_ZERO_BELOW

    _walk_key_blocks(step, q_start, tk)
    o_ref[0] = jnp.concatenate([acc_scr[h] for h in range(2 * n_pairs)], axis=0).T


PAIRS_PER_STEP = 2


def _attn_specs(b, tq_total, tk_total, width, tq):
    gw = PAIRS_PER_STEP * LANES
    grid = (b, width // gw, tq_total // tq)
    q_spec = pl.BlockSpec((1, tq, gw), lambda i, g, j: (i, j, g))
    kv_spec = pl.BlockSpec((1, tk_total, gw), lambda i, g, j: (i, 0, g))
    return grid, q_spec, kv_spec


def _attn_tiles(q, k, off):
    tq_total, tk_total = q.shape[1], k.shape[1]
    tq = 256 if tq_total % 256 == 0 else LANES
    tk = 256
    assert tq_total % tq == 0 and tk_total % tk == 0 and tk % tq == 0
    assert off % tq == 0 and off + tq_total <= tk_total
    return tq, tk


def _fox_attention(q, k, v, f, ft, kt, qt, qk_bound, off):
    b, tq_total, width = q.shape
    tk_total = k.shape[1]
    n_heads = width // HEAD_DIM
    tq, tk = _attn_tiles(q, k, off)
    grid, q_spec, kv_spec = _attn_specs(b, tq_total, tk_total, width, tq)
    f_end = jnp.transpose(f[:, tk - 1::tk, :n_heads], (0, 2, 1)).reshape(-1)
    qb = off // tq
    pp = PAIRS_PER_STEP
    return pl.pallas_call(
        functools.partial(_fox_kernel, tq=tq, tk=tk, off=off, n_heads=n_heads),
        grid=grid,
        in_specs=[_SMEM, _SMEM, q_spec, kv_spec, kv_spec,
                  pl.BlockSpec((1, pp, tk_total, LANES), lambda i, g, j: (i, g, 0, 0)),
                  pl.BlockSpec((1, pp, LANES, tq), lambda i, g, j: (i, g, 0, qb + j)),
                  pl.BlockSpec((1, 8, tq), lambda i, g, j: (i, 0, qb + j))],
        out_specs=q_spec,
        out_shape=jax.ShapeDtypeStruct((b, tq_total, width), F32),
        scratch_shapes=[pltpu.VMEM((2 * pp, tk_total, LANES), BF16),
                        pltpu.VMEM((pp * LANES, tk_total), BF16),
                        pltpu.VMEM((2 * pp, 1, tq), F32), pltpu.VMEM((2 * pp, 1, tq), F32),
                        pltpu.VMEM((2 * pp, HEAD_DIM, tq), F32)],
        compiler_params=_params(3),
        name="fox_attention",
    )(qk_bound, f_end, q, k, v, kt, qt, ft)


def _sb_attention(q, k, v, off):
    b, tq_total, width = q.shape
    tk_total = k.shape[1]
    tq, tk = _attn_tiles(q, k, off)
    grid, q_spec, kv_spec = _attn_specs(b, tq_total, tk_total, width, tq)
    r = lax.broadcasted_iota(jnp.int32, (tk, tk), 0)
    c = lax.broadcasted_iota(jnp.int32, (tk, tk), 1)
    tri = (c > r).astype(BF16)
    pp = PAIRS_PER_STEP
    return pl.pallas_call(
        functools.partial(_sb_kernel, tq=tq, tk=tk, off=off),
        grid=grid,
        in_specs=[q_spec, kv_spec, kv_spec, _resident(tri)],
        out_specs=q_spec,
        out_shape=jax.ShapeDtypeStruct((b, tq_total, width), F32),
        scratch_shapes=[pltpu.VMEM((tk_total, pp * LANES), BF16),
                        pltpu.VMEM((pp * LANES, tk_total), BF16),
                        pltpu.VMEM((2 * pp, 1, tq), F32), pltpu.VMEM((2 * pp, HEAD_DIM, tq), F32)],
        compiler_params=_params(3),
        name="sb_attention",
    )(q, k, v, tri)


def _out_proj_kernel(of_ref, os_ref, x_ref, gf_ref, gs_ref, w_ref, y_ref):
    a = _rms(of_ref[...], gf_ref[...]).astype(BF16)
    b = _rms(os_ref[...], gs_ref[...]).astype(BF16)
    nf = a.shape[1]
    y_ref[...] = x_ref[...] + _dot(a, w_ref[0:nf, :]) + _dot(b, w_ref[nf:, :])


def _out_proj(o_f, o_s, x2d, P):
    n, d = x2d.shape
    tm = min(512, n)
    row = lambda w: pl.BlockSpec((tm, w), lambda i: (i, 0))
    ins = [o_f, o_s, x2d, P["g_out_fox"], P["g_out_sb"], P["w_out"]]
    return pl.pallas_call(
        _out_proj_kernel,
        grid=(n // tm,),
        in_specs=[row(o_f.shape[1]), row(o_s.shape[1]), row(d)] + [_resident(a) for a in ins[3:]],
        out_specs=row(d),
        out_shape=jax.ShapeDtypeStruct((n, d), F32),
        compiler_params=_params(1),
        name="out_proj",
    )(*ins)


def _cross_kernel(x_ref, mk_ref, mv_ref, g_ref, wq_ref, gq_ref, wo_ref, y_ref):
    x = x_ref[0]
    h = _rms(x, g_ref[...]).astype(BF16)
    hd = wq_ref.shape[1] // MEM_HEADS
    scale = hd ** -0.5
    out = x
    for i in range(MEM_HEADS):
        sl = slice(i * hd, (i + 1) * hd)
        q = (_rms(_dot(h, wq_ref[:, sl]), gq_ref[...]) * scale).astype(BF16)
        s = _dot_nt(q, mk_ref[0, :, sl].astype(BF16))
        e = jnp.exp(s - jnp.max(s, axis=1, keepdims=True))
        o = _dot(e.astype(BF16), mv_ref[0, :, sl].astype(BF16)) / jnp.sum(e, axis=1, keepdims=True)
        out = out + _dot(o.astype(BF16), wo_ref[sl, :])
    y_ref[0] = out


def _cross_attention(x, mem_k, mem_v, P):
    b, t, d = x.shape
    tm = min(512, t)
    row = pl.BlockSpec((1, tm, d), lambda i, j: (i, j, 0))
    mem = pl.BlockSpec((1,) + mem_k.shape[1:], lambda i, j: (i, 0, 0))
    ins = [x, mem_k, mem_v, P["g_cross"], P["w_mq"], P["g_mq"], P["w_mo"]]
    return pl.pallas_call(
        _cross_kernel,
        grid=(b, t // tm),
        in_specs=[row, mem, mem] + [_resident(a) for a in ins[3:]],
        out_specs=row,
        out_shape=jax.ShapeDtypeStruct((b, t, d), F32),
        compiler_params=_params(2),
        name="cross_attention",
    )(*ins)


def _ffn_kernel(x_ref, g_ref, w1_ref, w2_ref, y_ref, *, chunk):
    x = x_ref[...]
    h = _rms(x, g_ref[...]).astype(BF16)
    acc = x
    for c in range(w1_ref.shape[1] // chunk):
        sl = slice(c * chunk, (c + 1) * chunk)
        u = jnp.square(jnp.maximum(_dot(h, w1_ref[:, sl]), 0.0)).astype(BF16)
        acc = acc + _dot(u, w2_ref[sl, :])
    y_ref[...] = acc


def _ffn(x2d, P):
    n, d = x2d.shape
    tm = min(512, n)
    row = pl.BlockSpec((tm, d), lambda i: (i, 0))
    once = lambda a: pl.BlockSpec(a.shape, lambda i: (0, 0), pipeline_mode=pl.Buffered(1))
    return pl.pallas_call(
        functools.partial(_ffn_kernel, chunk=1024),
        grid=(n // tm,),
        in_specs=[row, _resident(P["g_ffn"]), once(P["w_ff1"]), once(P["w_ff2"])],
        out_specs=row,
        out_shape=jax.ShapeDtypeStruct((n, d), F32),
        compiler_params=_params(1),
        name="ffn",
    )(x2d, P["g_ffn"], P["w_ff1"], P["w_ff2"])


def _mem_kv_kernel(m_ref, g_ref, wk_ref, wv_ref, gk_ref, k_ref, v_ref):
    h = _rms(m_ref[...], g_ref[...]).astype(BF16)
    hd = wk_ref.shape[1] // MEM_HEADS
    for i in range(MEM_HEADS):
        sl = slice(i * hd, (i + 1) * hd)
        k_ref[:, sl] = _rms(_dot(h, wk_ref[:, sl]), gk_ref[...])
    v_ref[...] = _dot(h, wv_ref[...])


def _memory_kv(mem2d, P):
    n, d = mem2d.shape
    w = P["w_mk"].shape[1]
    tm = min(256, n)
    row = lambda c: pl.BlockSpec((tm, c), lambda i: (i, 0))
    ins = [mem2d, P["g_mem"], P["w_mk"], P["w_mv"], P["g_mk"]]
    return pl.pallas_call(
        _mem_kv_kernel,
        grid=(n // tm,),
        in_specs=[row(d)] + [_resident(a) for a in ins[1:]],
        out_specs=[row(w), row(w)],
        out_shape=[jax.ShapeDtypeStruct((n, w), F32)] * 2,
        compiler_params=_params(1),
        name="memory_kv",
    )(*ins)


def _pad_rows(a, t):
    return jnp.pad(a, ((0, 0), (0, t - a.shape[1]), (0, 0)))


def _layer(x, mem_k, mem_v, past, P):
    b, t, d = x.shape
    x2d = x.reshape(b * t, d)
    q_f, k_f, v_f, q_s, k_s, v_s, lf = _proj_in(x2d, P)
    width = k_f.shape[1]
    new = [a.reshape(b, t, -1) for a in (k_f, v_f, lf, k_s, v_s)]
    q_f = q_f.reshape(b, t, width)
    q_s = q_s.reshape(b, t, width)
    if past is None:
        kf_all, vf_all, lf_all, ks_all, vs_all = new
        off, tq_total = 0, t
    else:
        off = past[0].shape[1]
        tq_total = -(-t // LANES) * LANES
        tk_total = -(-(off + tq_total) // 256) * 256
        past = [a.reshape(b, off, -1) for a in past]
        past[2] = jnp.pad(past[2].astype(F32), ((0, 0), (0, 0), (0, LANES - past[2].shape[2])))
        kf_all, vf_all, lf_all, ks_all, vs_all = [
            _pad_rows(jnp.concatenate([p, n], axis=1), tk_total) for p, n in zip(past, new)]
        q_f = _pad_rows(q_f, tq_total)
        q_s = _pad_rows(q_s, tq_total)
    tk_total = kf_all.shape[1]
    f, ft, kt, qt = _cumsum_logf(lf_all, 512 if tk_total % 512 == 0 else 256, width // LANES)
    o_f = _fox_attention(q_f, kf_all, vf_all, f, ft, kt, qt, P["qk_bound"], off)[:, :t]
    o_s = _sb_attention(q_s, ks_all, vs_all, off)[:, :t]
    y = _out_proj(o_f.reshape(b * t, width), o_s.reshape(b * t, width), x2d, P)
    y = _cross_attention(y.reshape(b, t, d), mem_k, mem_v, P)
    y = _ffn(y.reshape(b * t, d), P)
    return y.reshape(b, t, d), (k_f, v_f, lf[:, :P["n_fox_heads"]], k_s, v_s)


def _layer_params(l, g_mix, w_in, b_forget, g_fox_q, g_fox_k, g_out_fox, g_out_sb, w_out,
                  g_cross, g_mem, w_mq, w_mk, w_mv, g_mq, g_mk, w_mo, g_ffn, w_ff1, w_ff2):
    fox_w = g_out_fox.shape[1]
    sb_w = g_out_sb.shape[1]
    assert fox_w == sb_w
    n_heads = b_forget.shape[1]
    assert n_heads == 8 and g_fox_q.shape[1] == HEAD_DIM
    w = w_in[l]
    o_sb = 3 * fox_w + n_heads
    w_re = jnp.concatenate(
        [w[:, :3 * fox_w], w[:, o_sb:o_sb + 3 * sb_w],
         jnp.pad(w[:, 3 * fox_w:o_sb], ((0, 0), (0, LANES - n_heads)))], axis=1).astype(BF16)
    row = lambda a: a[l].reshape(1, -1).astype(F32)
    lane = jnp.arange(LANES) // HEAD_DIM
    qk_bound = (HEAD_DIM ** 0.5 * 1.01) * jnp.max(jnp.abs(g_fox_q[l])) * jnp.max(jnp.abs(g_fox_k[l]))
    return dict(
        n_fox_heads=n_heads, qk_bound=qk_bound.reshape(1).astype(F32),
        g_mix=row(g_mix), w_in=w_re,
        gq=jnp.tile(row(g_fox_q), (1, fox_w // HEAD_DIM)),
        gk=jnp.tile(row(g_fox_k), (1, fox_w // HEAD_DIM)),
        b_forget=jnp.pad(row(b_forget), ((0, 0), (0, LANES - n_heads))),
        grp=(lane[:, None] == lane[None, :]).astype(BF16),
        g_out_fox=row(g_out_fox), g_out_sb=row(g_out_sb), w_out=w_out[l].astype(BF16),
        g_cross=row(g_cross), w_mq=w_mq[l].astype(BF16), g_mq=row(g_mq), w_mo=w_mo[l].astype(BF16),
        g_mem=row(g_mem), w_mk=w_mk[l].astype(BF16), w_mv=w_mv[l].astype(BF16), g_mk=row(g_mk),
        g_ffn=row(g_ffn), w_ff1=w_ff1[l].astype(BF16), w_ff2=w_ff2[l].astype(BF16))


def kernel(x_prompt, x_sample, mem_prompt, cache_fox_k, cache_fox_v, cache_fox_logf, cache_sb_k, cache_sb_v, cache_mem_k, cache_mem_v, g_mix, w_in, b_forget, g_fox_q, g_fox_k, g_out_fox, g_out_sb, w_out, g_cross, g_mem, w_mq, w_mk, w_mv, g_mq, g_mk, w_mo, g_ffn, w_ff1, w_ff2):
    depth = g_mix.shape[0]
    bp, tp, d = x_prompt.shape
    bs, ts, _ = x_sample.shape
    n_mem = mem_prompt.shape[1]
    xp, xs = x_prompt, x_sample
    mem2d = mem_prompt.reshape(bp * n_mem, d)
    outs_p = [[] for _ in range(7)]
    outs_s = [[] for _ in range(5)]
    for l in range(depth):
        P = _layer_params(l, g_mix, w_in, b_forget, g_fox_q, g_fox_k, g_out_fox, g_out_sb, w_out,
                          g_cross, g_mem, w_mq, w_mk, w_mv, g_mq, g_mk, w_mo, g_ffn, w_ff1, w_ff2)
        n_heads = P["n_fox_heads"]
        mk, mv = _memory_kv(mem2d, P)
        mk = mk.reshape(bp, n_mem, -1)
        mv = mv.reshape(bp, n_mem, -1)
        xp, new_p = _layer(xp, mk, mv, None, P)
        for dst, a in zip(outs_p, new_p + (mk, mv)):
            dst.append(a)
        past = (cache_fox_k[l], cache_fox_v[l], cache_fox_logf[l], cache_sb_k[l], cache_sb_v[l])
        xs, new_s = _layer(xs, cache_mem_k[l].reshape(bs, n_mem, -1), cache_mem_v[l].reshape(bs, n_mem, -1),
                           past, P)
        for dst, a in zip(outs_s, new_s):
            dst.append(a)

    def heads(stack, b, t, hd):
        return jnp.stack(stack).reshape(depth, b, t, -1, hd)

    p_fk, p_fv, p_lf, p_sk, p_sv, p_mk, p_mv = outs_p
    s_fk, s_fv, s_lf, s_sk, s_sv = outs_s
    mem_hd = p_mk[0].shape[-1] // MEM_HEADS
    return (xp, xs,
            heads(p_fk, bp, tp, HEAD_DIM), heads(p_fv, bp, tp, HEAD_DIM),
            jnp.stack(p_lf).reshape(depth, bp, tp, n_heads),
            heads(p_sk, bp, tp, HEAD_DIM), heads(p_sv, bp, tp, HEAD_DIM),
            heads(p_mk, bp, n_mem, mem_hd), heads(p_mv, bp, n_mem, mem_hd),
            heads(s_fk, bs, ts, HEAD_DIM), heads(s_fv, bs, ts, HEAD_DIM),
            jnp.stack(s_lf).reshape(depth, bs, ts, n_heads),
            heads(s_sk, bs, ts, HEAD_DIM), heads(s_sv, bs, ts, HEAD_DIM))
```

```python
import functools
import math

import jax
import jax.numpy as jnp
from jax import lax
from jax.experimental import pallas as pl
from jax.experimental.pallas import tpu as pltpu

EPS = 1e-6
NEG = -1e30
HEAD_DIM = 64
MEM_HEADS = 4
LANES = 128
SUBLANES = 8
VMEM_LIMIT_BYTES = 56 * 2**20
LOG2E = math.log2(math.e)
EXP2_ZERO_BELOW = -106.0 * LOG2E
KEY_BLOCK = 256
PAIRS_PER_STEP = 2
F32 = jnp.float32
BF16 = jnp.bfloat16


def _dot(a, b):
    return jnp.dot(a, b, preferred_element_type=F32)


def _dot_nt(a, b):
    return lax.dot_general(a, b, (((1,), (1,)), ((), ())), preferred_element_type=F32)


def _rms(x, g):
    return x * lax.rsqrt(jnp.mean(x * x, axis=-1, keepdims=True) + EPS) * g


def _split3(x):
    hi = x.astype(BF16)
    r = x - hi.astype(F32)
    mid = r.astype(BF16)
    lo = (r - mid.astype(F32)).astype(BF16)
    return hi, mid, lo


def _params(n_grid_dims):
    return pltpu.CompilerParams(
        dimension_semantics=("arbitrary",) * n_grid_dims,
        vmem_limit_bytes=VMEM_LIMIT_BYTES)


def _resident(arr):
    nd = arr.ndim
    return pl.BlockSpec(arr.shape, lambda *_: (0,) * nd)


_SMEM = pl.BlockSpec(memory_space=pltpu.SMEM)
_ANY = pl.BlockSpec(memory_space=pl.ANY)


def _proj_in_kernel(*refs, width, n_prev):
    x_ref, g_ref, wt_ref, gq_ref, gk_ref, bf_ref = refs[:6]
    qf_ref, qs_ref, kf_ref, vf_ref, ks_ref, vs_ref, lf_ref = refs[6 + n_prev:]
    ht = _rms(x_ref[0], g_ref[...]).T.astype(BF16)
    q_scale = LOG2E * HEAD_DIM ** -0.5

    def seg(i, rows=width):
        return _dot(wt_ref[i * width:i * width + rows, :], ht)

    def head_norm(y, g_col, store):
        for h in range(width // HEAD_DIM):
            rows = slice(h * HEAD_DIM, (h + 1) * HEAD_DIM)
            yh = y[rows, :]
            ms = jnp.mean(yh * yh, axis=0, keepdims=True)
            store(rows, yh * lax.rsqrt(ms + EPS) * g_col[rows, :])

    def store_qf(rows, y):
        qf_ref[0, rows, :] = (y * q_scale).astype(BF16)

    def store_kf(rows, y):
        kf_ref[0, 0, rows, :] = y

    head_norm(seg(0), gq_ref[...], store_qf)
    head_norm(seg(1), gk_ref[...], store_kf)
    vf_ref[0, 0] = seg(2)
    qs_ref[0] = (seg(3) * q_scale).astype(BF16)
    ks_ref[0, 0] = seg(4)
    vs_ref[0, 0] = seg(5)
    n_gates = lf_ref.shape[2]
    lf_ref[0, 0] = jax.nn.log_sigmoid(seg(6, 2 * SUBLANES)[0:n_gates, :] + bf_ref[...])


def _proj_in(x, P, layer, depth, prev):
    b, t, d = x.shape
    width = P["gq"].shape[0]
    n_gates = P["b_forget"].shape[0]
    tm = min(256, t)
    n_prev = 0 if prev is None else len(prev)
    ins = [x, P["g_mix"], P["w_in_t"], P["gq"], P["gk"], P["b_forget"]]
    q_spec = pl.BlockSpec((1, width, tm), lambda i, j: (i, 0, j))
    kv_spec = pl.BlockSpec((1, 1, width, tm), lambda i, j: (layer, i, 0, j))
    lf_spec = pl.BlockSpec((1, 1, n_gates, tm), lambda i, j: (layer, i, 0, j))
    sds = jax.ShapeDtypeStruct
    kv_shape = sds((depth, b, width, t), F32)
    outs = pl.pallas_call(
        functools.partial(_proj_in_kernel, width=width, n_prev=n_prev),
        grid=(b, t // tm),
        in_specs=[pl.BlockSpec((1, tm, d), lambda i, j: (i, j, 0))]
                 + [_resident(a) for a in ins[1:]] + [_ANY] * n_prev,
        out_specs=[q_spec, q_spec] + [kv_spec] * 4 + [lf_spec],
        out_shape=[sds((b, width, t), BF16)] * 2 + [kv_shape] * 4 + [sds((depth, b, n_gates, t), F32)],
        input_output_aliases={len(ins) + i: 2 + i for i in range(n_prev)},
        compiler_params=_params(2),
        name="proj_in",
    )(*ins, *(prev or ()))
    return outs[0], outs[1], tuple(outs[2:])


def _cumsum_kernel(lf_ref, tri_ref, f_ref, carry_ref):
    @pl.when(pl.program_id(1) == 0)
    def _():
        carry_ref[...] = jnp.zeros_like(carry_ref)

    tri = tri_ref[...]
    f = sum(_dot(part, tri) for part in _split3(lf_ref[0])) + carry_ref[...]
    f_ref[0] = f * LOG2E
    tb = f.shape[1]
    carry_ref[...] = f[:, tb - 1:tb]


def _cumsum_logf(lf_t):
    b, n_gates, t = lf_t.shape
    tb = 512 if t % 512 == 0 else KEY_BLOCK
    r = lax.broadcasted_iota(jnp.int32, (tb, tb), 0)
    c = lax.broadcasted_iota(jnp.int32, (tb, tb), 1)
    tri = (r <= c).astype(BF16)
    spec = pl.BlockSpec((1, n_gates, tb), lambda i, j: (i, 0, j))
    return pl.pallas_call(
        _cumsum_kernel,
        grid=(b, t // tb),
        in_specs=[spec, _resident(tri)],
        out_specs=spec,
        out_shape=jax.ShapeDtypeStruct((b, n_gates, t), F32),
        scratch_shapes=[pltpu.VMEM((n_gates, 1), F32)],
        compiler_params=_params(2),
        name="cumsum_logf",
    )(lf_t, tri)


def _key_chunks(n_past, n_new, tk):
    assert n_past % tk == 0
    new_src = 1 if n_past else 0
    chunks = [(0, s, tk, s) for s in range(0, n_past, tk)]
    chunks += [(new_src, s, min(tk, n_new - s), n_past + s) for s in range(0, n_new, tk)]
    return chunks


def _positions(q_start, ks, tq, tk):
    kpos = ks + lax.broadcasted_iota(jnp.int32, (tk, tq), 0)
    qpos = q_start + lax.broadcasted_iota(jnp.int32, (tk, tq), 1)
    return kpos, qpos


def _walk_key_blocks(step, q_start, tk):
    kb_diag = q_start // tk
    go = step(kb_diag, True)

    def cond(carry):
        kb, go = carry
        return jnp.logical_and(kb >= 0, go)

    def body(carry):
        kb, _ = carry
        return kb - 1, step(kb, False)

    lax.while_loop(cond, body, (kb_diag - 1, go))


def _pick_row(block, idx):
    row = lax.broadcasted_iota(jnp.int32, block.shape, 0)
    return jnp.sum(jnp.where(row == idx, block, 0.0), axis=0, keepdims=True)


def _gate_rows(f_row, key_side):
    n = f_row.shape[1]
    parts = [p.astype(F32) for p in _split3(-f_row if key_side else f_row)]
    ones = [jnp.ones((1, n), F32)] * 3
    row = lax.broadcasted_iota(jnp.int32, (SUBLANES, n), 0)
    out = jnp.zeros((SUBLANES, n), F32)
    for i, v in enumerate(parts + ones if key_side else ones + parts):
        out = jnp.where(row == i, v, out)
    return out


def _with_gate_rows(x_t, gates, head):
    n = x_t.shape[1]
    fill = jnp.zeros((HEAD_DIM - SUBLANES, n), F32)
    if head == 0:
        return jnp.concatenate([x_t[:HEAD_DIM], gates, fill], axis=0)
    return jnp.concatenate([gates, fill, x_t[HEAD_DIM:]], axis=0)


def _fox_kernel(*refs, tq, tk, off, n_heads, chunks):
    n_src = 1 + max(c[0] for c in chunks)
    bound_ref, fend_ref, q_ref = refs[:3]
    kv_refs = refs[3:3 + 2 * n_src]
    f_ref, o_ref, k_scr, vt_scr, m_scr, l_scr, acc_scr = refs[3 + 2 * n_src:]
    b = pl.program_id(0)
    group = pl.program_id(1)
    qi = pl.program_id(2)
    n_kb = vt_scr.shape[1] // tk
    n_pairs = vt_scr.shape[0] // LANES
    first_head = 2 * n_pairs * group

    @pl.when(qi == 0)
    def _():
        filled = max(dst + w for _, _, w, dst in chunks)
        if filled < n_kb * tk:
            k_scr[:, filled:, :] = jnp.zeros((2 * n_pairs, n_kb * tk - filled, LANES), BF16)
            vt_scr[:, filled:] = jnp.zeros((n_pairs * LANES, n_kb * tk - filled), BF16)
        for src, start, w, dst in chunks:
            k_ref, v_ref = kv_refs[2 * src], kv_refs[2 * src + 1]
            vt_scr[:, dst:dst + w] = v_ref[0, 0, :, start:start + w].astype(BF16)
            for p in range(n_pairs):
                k_t = k_ref[0, 0, p * LANES:(p + 1) * LANES, start:start + w]
                for j in range(2):
                    gates = _gate_rows(_pick_row(f_ref[0, :, dst:dst + w], first_head + 2 * p + j), True)
                    k_scr[2 * p + j, dst:dst + w, :] = _with_gate_rows(k_t, gates, j).T.astype(BF16)

    q_start = off + qi * tq
    qs, fqs = [], []
    for p in range(n_pairs):
        q_t = q_ref[0, p * LANES:(p + 1) * LANES, :].astype(F32)
        for j in range(2):
            fq = _pick_row(f_ref[0, :, pl.ds(pl.multiple_of(q_start, LANES), tq)], first_head + 2 * p + j)
            fqs.append(fq)
            qs.append(_with_gate_rows(q_t, _gate_rows(fq, False), j).astype(BF16))

    m_scr[...] = jnp.full_like(m_scr, NEG)
    l_scr[...] = jnp.zeros_like(l_scr)
    acc_scr[...] = jnp.zeros_like(acc_scr)

    def step(kb, masked):
        ks = pl.multiple_of(kb * tk, tk)
        if masked:
            kpos, qpos = _positions(q_start, ks, tq, tk)
            visible = kpos <= qpos
        scores = [_dot(k_scr[h, pl.ds(ks, tk), :], qs[h]) for h in range(2 * n_pairs)]
        go = False
        for h in range(2 * n_pairs):
            s = scores[h]
            if masked:
                s = jnp.where(visible, s, NEG)
            m_prev = m_scr[h]
            m_new = jnp.maximum(m_prev, jnp.max(s, axis=0, keepdims=True))
            alpha = jnp.exp2(m_prev - m_new)
            p = jnp.exp2(s - m_new)
            l_scr[h] = alpha * l_scr[h] + jnp.sum(p, axis=0, keepdims=True)
            m_scr[h] = m_new
            vt = vt_scr[h * HEAD_DIM:(h + 1) * HEAD_DIM, pl.ds(ks, tk)]
            acc_scr[h] = acc_scr[h] * alpha + _dot(vt, p.astype(BF16))
            f_end = fend_ref[(b * n_heads + first_head + h) * n_kb + jnp.maximum(kb - 1, 0)]
            reach = jnp.max(fqs[h] - m_new) + bound_ref[0] - f_end
            go = jnp.logical_or(go, reach >= EXP2_ZERO_BELOW)
        return go

    _walk_key_blocks(step, q_start, tk)
    o_t = jnp.concatenate([acc_scr[h] / l_scr[h] for h in range(2 * n_pairs)], axis=0)
    o_ref[0] = o_t.T


def _sb_kernel(*refs, tq, tk, off, chunks):
    n_src = 1 + max(c[0] for c in chunks)
    q_ref = refs[0]
    kv_refs = refs[1:1 + 2 * n_src]
    tri_ref, o_ref, kb_scr, vt_scr, c_scr, acc_scr = refs[1 + 2 * n_src:]
    qi = pl.program_id(2)
    n_kb = vt_scr.shape[1] // tk
    n_pairs = vt_scr.shape[0] // LANES
    row_lo = lax.broadcasted_iota(jnp.int32, (LANES, 1), 0) < HEAD_DIM

    @pl.when(qi == 0)
    def _():
        filled = max(dst + w for _, _, w, dst in chunks)
        if filled < n_kb * tk:
            kb_scr[filled:, :] = jnp.zeros((n_kb * tk - filled, n_pairs * LANES), BF16)
            vt_scr[:, filled:] = jnp.zeros((n_pairs * LANES, n_kb * tk - filled), BF16)
        for src, start, w, dst in chunks:
            k_ref, v_ref = kv_refs[2 * src], kv_refs[2 * src + 1]
            vt_scr[:, dst:dst + w] = v_ref[0, 0, :, start:start + w].astype(BF16)
            for p in range(n_pairs):
                lanes = slice(p * LANES, (p + 1) * LANES)
                kb_scr[dst:dst + w, lanes] = k_ref[0, 0, lanes, start:start + w].T.astype(BF16)

    qs = []
    for p in range(n_pairs):
        q_t = q_ref[0, p * LANES:(p + 1) * LANES, :]
        zero = jnp.zeros_like(q_t)
        qs += [jnp.where(row_lo, q_t, zero), jnp.where(row_lo, zero, q_t)]
    c_scr[...] = jnp.zeros_like(c_scr)
    acc_scr[...] = jnp.zeros_like(acc_scr)
    q_start = off + qi * tq

    def step(kb, masked):
        ks = pl.multiple_of(kb * tk, tk)
        if masked:
            kpos, qpos = _positions(q_start, ks, tq, tk)
            strict = kpos < qpos
        zs = [_dot(kb_scr[pl.ds(ks, tk), (h // 2) * LANES:(h // 2 + 1) * LANES], qs[h])
              for h in range(2 * n_pairs)]
        reach = None
        for h in range(2 * n_pairs):
            z = zs[h]
            log_b = jnp.minimum(z, 0.0) - jnp.log2(1.0 + jnp.exp2(-jnp.abs(z)))
            log_1m = log_b - z
            if masked:
                log_1m = jnp.where(strict, log_1m, 0.0)
            rest = _dot(tri_ref[...], log_1m.astype(BF16)) + c_scr[h]
            arg = log_b + rest
            if masked:
                arg = jnp.where(strict, arg, NEG)
            a = jnp.exp2(arg)
            c_new = c_scr[h] + jnp.sum(log_1m, axis=0, keepdims=True)
            c_scr[h] = c_new
            vt = vt_scr[h * HEAD_DIM:(h + 1) * HEAD_DIM, pl.ds(ks, tk)]
            acc_scr[h] = acc_scr[h] + _dot(vt, a.astype(BF16))
            reach = c_new if reach is None else jnp.maximum(reach, c_new)
        return jnp.max(reach) >= EXP2_ZERO_BELOW

    _walk_key_blocks(step, q_start, tk)
    o_ref[0] = jnp.concatenate([acc_scr[h] for h in range(2 * n_pairs)], axis=0).T


def _attn_geometry(q_t, new, past, layer):
    b, width, tq_total = q_t.shape
    n_new = new[0].shape[3]
    n_past = 0 if past is None else past[0].shape[3]
    tk = KEY_BLOCK
    tq = 256 if tq_total % 256 == 0 else LANES
    assert tq_total % tq == 0 and tk % tq == 0 and n_past % tq == 0 and n_new >= tq_total
    chunks = _key_chunks(n_past, n_new, tk)
    tk_total = -(-(n_past + n_new) // tk) * tk
    gw = PAIRS_PER_STEP * LANES
    grid = (b, width // gw, tq_total // tq)
    q_spec = pl.BlockSpec((1, gw, tq), lambda i, g, j: (i, g, j))
    o_spec = pl.BlockSpec((1, tq, gw), lambda i, g, j: (i, j, g))
    kv_ins, kv_specs = [], []
    for src in ([past] if past is not None else []) + [new]:
        n = src[0].shape[3]
        kv_ins += list(src)
        kv_specs += [pl.BlockSpec((1, 1, gw, n), lambda i, g, j: (layer, i, g, 0))] * 2
    return dict(b=b, width=width, tq_total=tq_total, tq=tq, tk=tk, off=n_past, chunks=chunks,
                tk_total=tk_total, gw=gw, grid=grid, q_spec=q_spec, o_spec=o_spec,
                kv_ins=kv_ins, kv_specs=kv_specs)


def _fox_attention(q_t, new, past, layer, f2, qk_bound):
    G = _attn_geometry(q_t, new, past, layer)
    b, tq, tk, tk_total, gw = G["b"], G["tq"], G["tk"], G["tk_total"], G["gw"]
    n_heads = G["width"] // HEAD_DIM
    assert f2.shape == (b, n_heads, tk_total)
    f_end = f2[:, :, tk - 1::tk].reshape(-1)
    heads = gw // HEAD_DIM
    return pl.pallas_call(
        functools.partial(_fox_kernel, tq=tq, tk=tk, off=G["off"], n_heads=n_heads, chunks=G["chunks"]),
        grid=G["grid"],
        in_specs=[_SMEM, _SMEM, G["q_spec"]] + G["kv_specs"]
                 + [pl.BlockSpec((1, n_heads, tk_total), lambda i, g, j: (i, 0, 0))],
        out_specs=G["o_spec"],
        out_shape=jax.ShapeDtypeStruct((b, G["tq_total"], G["width"]), F32),
        scratch_shapes=[pltpu.VMEM((heads, tk_total, LANES), BF16),
                        pltpu.VMEM((gw, tk_total), BF16),
                        pltpu.VMEM((heads, 1, tq), F32), pltpu.VMEM((heads, 1, tq), F32),
                        pltpu.VMEM((heads, HEAD_DIM, tq), F32)],
        compiler_params=_params(3),
        name="fox_attention",
    )(qk_bound, f_end, q_t, *G["kv_ins"], f2)


def _sb_attention(q_t, new, past, layer):
    G = _attn_geometry(q_t, new, past, layer)
    b, tq, tk, tk_total, gw = G["b"], G["tq"], G["tk"], G["tk_total"], G["gw"]
    r = lax.broadcasted_iota(jnp.int32, (tk, tk), 0)
    c = lax.broadcasted_iota(jnp.int32, (tk, tk), 1)
    tri = (c > r).astype(BF16)
    heads = gw // HEAD_DIM
    return pl.pallas_call(
        functools.partial(_sb_kernel, tq=tq, tk=tk, off=G["off"], chunks=G["chunks"]),
        grid=G["grid"],
        in_specs=[G["q_spec"]] + G["kv_specs"] + [_resident(tri)],
        out_specs=G["o_spec"],
        out_shape=jax.ShapeDtypeStruct((b, G["tq_total"], G["width"]), F32),
        scratch_shapes=[pltpu.VMEM((tk_total, gw), BF16), pltpu.VMEM((gw, tk_total), BF16),
                        pltpu.VMEM((heads, 1, tq), F32), pltpu.VMEM((heads, HEAD_DIM, tq), F32)],
        compiler_params=_params(3),
        name="sb_attention",
    )(q_t, *G["kv_ins"], tri)


def _out_proj_kernel(of_ref, os_ref, x_ref, gf_ref, gs_ref, w_ref, y_ref):
    a = _rms(of_ref[...], gf_ref[...]).astype(BF16)
    b = _rms(os_ref[...], gs_ref[...]).astype(BF16)
    nf = a.shape[1]
    y_ref[...] = x_ref[...] + _dot(a, w_ref[0:nf, :]) + _dot(b, w_ref[nf:, :])


def _out_proj(o_f, o_s, x2d, P):
    n, d = x2d.shape
    tm = min(512, n)
    row = lambda w: pl.BlockSpec((tm, w), lambda i: (i, 0))
    ins = [o_f, o_s, x2d, P["g_out_fox"], P["g_out_sb"], P["w_out"]]
    return pl.pallas_call(
        _out_proj_kernel,
        grid=(n // tm,),
        in_specs=[row(o_f.shape[1]), row(o_s.shape[1]), row(d)] + [_resident(a) for a in ins[3:]],
        out_specs=row(d),
        out_shape=jax.ShapeDtypeStruct((n, d), F32),
        compiler_params=_params(1),
        name="out_proj",
    )(*ins)


def _cross_kernel(x_ref, mk_ref, mv_ref, g_ref, wq_ref, gq_ref, wo_ref, y_ref):
    x = x_ref[0]
    h = _rms(x, g_ref[...]).astype(BF16)
    hd = wq_ref.shape[1] // MEM_HEADS
    scale = hd ** -0.5
    out = x
    for i in range(MEM_HEADS):
        sl = slice(i * hd, (i + 1) * hd)
        q = (_rms(_dot(h, wq_ref[:, sl]), gq_ref[...]) * scale).astype(BF16)
        s = _dot_nt(q, mk_ref[0, :, sl].astype(BF16))
        e = jnp.exp(s - jnp.max(s, axis=1, keepdims=True))
        o = _dot(e.astype(BF16), mv_ref[0, :, sl].astype(BF16)) / jnp.sum(e, axis=1, keepdims=True)
        out = out + _dot(o.astype(BF16), wo_ref[sl, :])
    y_ref[0] = out


def _cross_attention(x, mem_k, mem_v, P):
    b, t, d = x.shape
    tm = min(512, t)
    row = pl.BlockSpec((1, tm, d), lambda i, j: (i, j, 0))
    mem = pl.BlockSpec((1,) + mem_k.shape[1:], lambda i, j: (i, 0, 0))
    ins = [x, mem_k, mem_v, P["g_cross"], P["w_mq"], P["g_mq"], P["w_mo"]]
    return pl.pallas_call(
        _cross_kernel,
        grid=(b, t // tm),
        in_specs=[row, mem, mem] + [_resident(a) for a in ins[3:]],
        out_specs=row,
        out_shape=jax.ShapeDtypeStruct((b, t, d), F32),
        compiler_params=_params(2),
        name="cross_attention",
    )(*ins)


def _ffn_kernel(x_ref, g_ref, w1_ref, w2_ref, y_ref, *, chunk):
    x = x_ref[...]
    h = _rms(x, g_ref[...]).astype(BF16)
    acc = x
    for c in range(w1_ref.shape[1] // chunk):
        sl = slice(c * chunk, (c + 1) * chunk)
        u = jnp.square(jnp.maximum(_dot(h, w1_ref[:, sl]), 0.0)).astype(BF16)
        acc = acc + _dot(u, w2_ref[sl, :])
    y_ref[...] = acc


def _ffn(x2d, P):
    n, d = x2d.shape
    tm = min(512, n)
    row = pl.BlockSpec((tm, d), lambda i: (i, 0))
    once = lambda a: pl.BlockSpec(a.shape, lambda i: (0, 0), pipeline_mode=pl.Buffered(1))
    return pl.pallas_call(
        functools.partial(_ffn_kernel, chunk=1024),
        grid=(n // tm,),
        in_specs=[row, _resident(P["g_ffn"]), once(P["w_ff1"]), once(P["w_ff2"])],
        out_specs=row,
        out_shape=jax.ShapeDtypeStruct((n, d), F32),
        compiler_params=_params(1),
        name="ffn",
    )(x2d, P["g_ffn"], P["w_ff1"], P["w_ff2"])


def _mem_kv_kernel(m_ref, g_ref, wk_ref, wv_ref, gk_ref, k_ref, v_ref):
    h = _rms(m_ref[...], g_ref[...]).astype(BF16)
    hd = wk_ref.shape[1] // MEM_HEADS
    for i in range(MEM_HEADS):
        sl = slice(i * hd, (i + 1) * hd)
        k_ref[:, sl] = _rms(_dot(h, wk_ref[:, sl]), gk_ref[...])
    v_ref[...] = _dot(h, wv_ref[...])


def _memory_kv(mem2d, P):
    n, d = mem2d.shape
    w = P["w_mk"].shape[1]
    tm = min(256, n)
    row = lambda c: pl.BlockSpec((tm, c), lambda i: (i, 0))
    ins = [mem2d, P["g_mem"], P["w_mk"], P["w_mv"], P["g_mk"]]
    return pl.pallas_call(
        _mem_kv_kernel,
        grid=(n // tm,),
        in_specs=[row(d)] + [_resident(a) for a in ins[1:]],
        out_specs=[row(w), row(w)],
        out_shape=[jax.ShapeDtypeStruct((n, w), F32)] * 2,
        compiler_params=_params(1),
        name="memory_kv",
    )(*ins)


def _layer(x, mem_k, mem_v, past, P, layer, depth, stacks):
    b, t, d = x.shape
    qf_t, qs_t, stacks = _proj_in(x, P, layer, depth, stacks)
    kf, vf, ks, vs, lf = stacks
    lf_t = lf[layer]
    if past is not None:
        lf_t = jnp.concatenate([past[2][layer], lf_t], axis=2)
    pad = -lf_t.shape[2] % KEY_BLOCK
    f2 = _cumsum_logf(jnp.pad(lf_t, ((0, 0), (0, 0), (0, pad))))
    o_f = _fox_attention(qf_t, (kf, vf), None if past is None else (past[0], past[1]), layer, f2, P["qk_bound"])
    o_s = _sb_attention(qs_t, (ks, vs), None if past is None else (past[3], past[4]), layer)
    y = _out_proj(o_f.reshape(b * t, -1), o_s.reshape(b * t, -1), x.reshape(b * t, d), P)
    y = _cross_attention(y.reshape(b, t, d), mem_k, mem_v, P)
    y = _ffn(y.reshape(b * t, d), P)
    return y.reshape(b, t, d), stacks


def _layer_params(l, g_mix, w_in, b_forget, g_fox_q, g_fox_k, g_out_fox, g_out_sb, w_out,
                  g_cross, g_mem, w_mq, w_mk, w_mv, g_mq, g_mk, w_mo, g_ffn, w_ff1, w_ff2):
    fox_w = g_out_fox.shape[1]
    sb_w = g_out_sb.shape[1]
    n_heads = b_forget.shape[1]
    assert fox_w == sb_w and n_heads == SUBLANES and g_fox_q.shape[1] == HEAD_DIM
    w = w_in[l]
    o_sb = 3 * fox_w + n_heads
    w_t = jnp.concatenate(
        [w[:, :3 * fox_w], w[:, o_sb:o_sb + 3 * sb_w],
         jnp.pad(w[:, 3 * fox_w:o_sb], ((0, 0), (0, 2 * SUBLANES - n_heads)))], axis=1).T.astype(BF16)
    row = lambda a: a[l].reshape(1, -1).astype(F32)
    col = lambda a: a[l].reshape(-1, 1).astype(F32)
    qk_bound = (LOG2E * HEAD_DIM ** 0.5 * 1.01) * jnp.max(jnp.abs(g_fox_q[l])) * jnp.max(jnp.abs(g_fox_k[l]))
    return dict(
        qk_bound=qk_bound.reshape(1).astype(F32),
        g_mix=row(g_mix), w_in_t=w_t,
        gq=jnp.tile(col(g_fox_q), (fox_w // HEAD_DIM, 1)),
        gk=jnp.tile(col(g_fox_k), (fox_w // HEAD_DIM, 1)),
        b_forget=col(b_forget),
        g_out_fox=row(g_out_fox), g_out_sb=row(g_out_sb), w_out=w_out[l].astype(BF16),
        g_cross=row(g_cross), w_mq=w_mq[l].astype(BF16), g_mq=row(g_mq), w_mo=w_mo[l].astype(BF16),
        g_mem=row(g_mem), w_mk=w_mk[l].astype(BF16), w_mv=w_mv[l].astype(BF16), g_mk=row(g_mk),
        g_ffn=row(g_ffn), w_ff1=w_ff1[l].astype(BF16), w_ff2=w_ff2[l].astype(BF16))


def _feature_major(cache):
    depth, b, t = cache.shape[:3]
    return jnp.moveaxis(cache.reshape(depth, b, t, -1), 2, 3)


def _time_major(stack, t, hd):
    depth, b, w, _ = stack.shape
    return jnp.moveaxis(stack[..., :t].reshape(depth, b, w // hd, hd, t), 4, 2)


def kernel(x_prompt, x_sample, mem_prompt, cache_fox_k, cache_fox_v, cache_fox_logf, cache_sb_k, cache_sb_v, cache_mem_k, cache_mem_v, g_mix, w_in, b_forget, g_fox_q, g_fox_k, g_out_fox, g_out_sb, w_out, g_cross, g_mem, w_mq, w_mk, w_mv, g_mq, g_mk, w_mo, g_ffn, w_ff1, w_ff2):
    depth = g_mix.shape[0]
    bp, tp, d = x_prompt.shape
    bs, ts, _ = x_sample.shape
    n_mem = mem_prompt.shape[1]
    n_heads = b_forget.shape[1]
    ts_pad = -(-ts // LANES) * LANES
    xp = x_prompt
    xs = jnp.pad(x_sample, ((0, 0), (0, ts_pad - ts), (0, 0)))
    mem2d = mem_prompt.reshape(bp * n_mem, d)
    past = (_feature_major(cache_fox_k), _feature_major(cache_fox_v),
            jnp.moveaxis(cache_fox_logf.astype(F32), 2, 3),
            _feature_major(cache_sb_k), _feature_major(cache_sb_v))
    stacks_p = stacks_s = None
    p_mk, p_mv = [], []
    for l in range(depth):
        P = _layer_params(l, g_mix, w_in, b_forget, g_fox_q, g_fox_k, g_out_fox, g_out_sb, w_out,
                          g_cross, g_mem, w_mq, w_mk, w_mv, g_mq, g_mk, w_mo, g_ffn, w_ff1, w_ff2)
        mk, mv = _memory_kv(mem2d, P)
        mk = mk.reshape(bp, n_mem, -1)
        mv = mv.reshape(bp, n_mem, -1)
        p_mk.append(mk)
        p_mv.append(mv)
        xp, stacks_p = _layer(xp, mk, mv, None, P, l, depth, stacks_p)
        xs, stacks_s = _layer(xs, cache_mem_k[l].reshape(bs, n_mem, -1), cache_mem_v[l].reshape(bs, n_mem, -1),
                              past, P, l, depth, stacks_s)

    mem_hd = p_mk[0].shape[-1] // MEM_HEADS
    p_fk, p_fv, p_sk, p_sv, p_lf = stacks_p
    s_fk, s_fv, s_sk, s_sv, s_lf = stacks_s
    return (xp, xs[:, :ts],
            _time_major(p_fk, tp, HEAD_DIM), _time_major(p_fv, tp, HEAD_DIM),
            jnp.moveaxis(p_lf, 2, 3),
            _time_major(p_sk, tp, HEAD_DIM), _time_major(p_sv, tp, HEAD_DIM),
            jnp.stack(p_mk).reshape(depth, bp, n_mem, MEM_HEADS, mem_hd),
            jnp.stack(p_mv).reshape(depth, bp, n_mem, MEM_HEADS, mem_hd),
            _time_major(s_fk, ts, HEAD_DIM), _time_major(s_fv, ts, HEAD_DIM),
            jnp.moveaxis(s_lf[..., :ts], 2, 3),
            _time_major(s_sk, ts, HEAD_DIM), _time_major(s_sv, ts, HEAD_DIM))
```

```python
import functools
import math

import jax
import jax.numpy as jnp
from jax import lax
from jax.experimental import pallas as pl
from jax.experimental.pallas import tpu as pltpu

EPS = 1e-6
NEG = -1e30
HEAD_DIM = 64
MEM_HEADS = 4
LANES = 128
SUBLANES = 8
VMEM_LIMIT_BYTES = 56 * 2**20
LOG2E = math.log2(math.e)
EXP2_ZERO_BELOW = -106.0 * LOG2E
KEY_BLOCK = 256
PAIRS_PER_STEP = 4
F32 = jnp.float32
BF16 = jnp.bfloat16


def _dot(a, b):
    return jnp.dot(a, b, preferred_element_type=F32)


def _dot_nt(a, b):
    return lax.dot_general(a, b, (((1,), (1,)), ((), ())), preferred_element_type=F32)


def _rms(x, g):
    return x * lax.rsqrt(jnp.mean(x * x, axis=-1, keepdims=True) + EPS) * g


def _split3(x):
    hi = x.astype(BF16)
    r = x - hi.astype(F32)
    mid = r.astype(BF16)
    lo = (r - mid.astype(F32)).astype(BF16)
    return hi, mid, lo


def _params(n_grid_dims):
    return pltpu.CompilerParams(
        dimension_semantics=("arbitrary",) * n_grid_dims,
        vmem_limit_bytes=VMEM_LIMIT_BYTES)


def _resident(arr):
    nd = arr.ndim
    return pl.BlockSpec(arr.shape, lambda *_: (0,) * nd)


_SMEM = pl.BlockSpec(memory_space=pltpu.SMEM)
_ANY = pl.BlockSpec(memory_space=pl.ANY)


def _proj_in_kernel(*refs, width, n_prev):
    x_ref, g_ref, wt_ref, gq_ref, gk_ref, bf_ref = refs[:6]
    qf_ref, qs_ref, kf_ref, vf_ref, ks_ref, vs_ref, lf_ref = refs[6 + n_prev:]
    ht = _rms(x_ref[0], g_ref[...]).T.astype(BF16)
    q_scale = LOG2E * HEAD_DIM ** -0.5

    def seg(i, rows=width):
        return _dot(wt_ref[i * width:i * width + rows, :], ht)

    def head_norm(y, g_col, store):
        for h in range(width // HEAD_DIM):
            rows = slice(h * HEAD_DIM, (h + 1) * HEAD_DIM)
            yh = y[rows, :]
            ms = jnp.mean(yh * yh, axis=0, keepdims=True)
            store(rows, yh * lax.rsqrt(ms + EPS) * g_col[rows, :])

    def store_qf(rows, y):
        qf_ref[0, rows, :] = (y * q_scale).astype(BF16)

    def store_kf(rows, y):
        kf_ref[0, 0, rows, :] = y

    head_norm(seg(0), gq_ref[...], store_qf)
    head_norm(seg(1), gk_ref[...], store_kf)
    vf_ref[0, 0] = seg(2)
    qs_ref[0] = (seg(3) * q_scale).astype(BF16)
    ks_ref[0, 0] = seg(4)
    vs_ref[0, 0] = seg(5)
    n_gates = lf_ref.shape[2]
    lf_ref[0, 0] = jax.nn.log_sigmoid(seg(6, 2 * SUBLANES)[0:n_gates, :] + bf_ref[...])


def _proj_in(x, P, layer, depth, prev):
    b, t, d = x.shape
    width = P["gq"].shape[0]
    n_gates = P["b_forget"].shape[0]
    tm = min(512, t)
    n_prev = 0 if prev is None else len(prev)
    ins = [x, P["g_mix"], P["w_in_t"], P["gq"], P["gk"], P["b_forget"]]
    q_spec = pl.BlockSpec((1, width, tm), lambda i, j: (i, 0, j))
    kv_spec = pl.BlockSpec((1, 1, width, tm), lambda i, j: (layer, i, 0, j))
    lf_spec = pl.BlockSpec((1, 1, n_gates, tm), lambda i, j: (layer, i, 0, j))
    sds = jax.ShapeDtypeStruct
    kv_shape = sds((depth, b, width, t), F32)
    outs = pl.pallas_call(
        functools.partial(_proj_in_kernel, width=width, n_prev=n_prev),
        grid=(b, t // tm),
        in_specs=[pl.BlockSpec((1, tm, d), lambda i, j: (i, j, 0))]
                 + [_resident(a) for a in ins[1:]] + [_ANY] * n_prev,
        out_specs=[q_spec, q_spec] + [kv_spec] * 4 + [lf_spec],
        out_shape=[sds((b, width, t), BF16)] * 2 + [kv_shape] * 4 + [sds((depth, b, n_gates, t), F32)],
        input_output_aliases={len(ins) + i: 2 + i for i in range(n_prev)},
        compiler_params=_params(2),
        name="proj_in",
    )(*ins, *(prev or ()))
    return outs[0], outs[1], tuple(outs[2:])


def _cumsum_kernel(lf_ref, tri_ref, f_ref, carry_ref):
    @pl.when(pl.program_id(1) == 0)
    def _():
        carry_ref[...] = jnp.zeros_like(carry_ref)

    tri = tri_ref[...]
    f = sum(_dot(part, tri) for part in _split3(lf_ref[0])) + carry_ref[...]
    f_ref[0] = f * LOG2E
    tb = f.shape[1]
    carry_ref[...] = f[:, tb - 1:tb]


def _cumsum_logf(lf_t):
    b, n_gates, t = lf_t.shape
    tb = 512 if t % 512 == 0 else KEY_BLOCK
    r = lax.broadcasted_iota(jnp.int32, (tb, tb), 0)
    c = lax.broadcasted_iota(jnp.int32, (tb, tb), 1)
    tri = (r <= c).astype(BF16)
    spec = pl.BlockSpec((1, n_gates, tb), lambda i, j: (i, 0, j))
    return pl.pallas_call(
        _cumsum_kernel,
        grid=(b, t // tb),
        in_specs=[spec, _resident(tri)],
        out_specs=spec,
        out_shape=jax.ShapeDtypeStruct((b, n_gates, t), F32),
        scratch_shapes=[pltpu.VMEM((n_gates, 1), F32)],
        compiler_params=_params(2),
        name="cumsum_logf",
    )(lf_t, tri)


def _key_chunks(n_past, n_new, tk):
    assert n_past % tk == 0
    new_src = 1 if n_past else 0
    chunks = [(0, s, tk, s) for s in range(0, n_past, tk)]
    chunks += [(new_src, s, min(tk, n_new - s), n_past + s) for s in range(0, n_new, tk)]
    return chunks


def _positions(q_start, ks, tq, tk):
    kpos = ks + lax.broadcasted_iota(jnp.int32, (tk, tq), 0)
    qpos = q_start + lax.broadcasted_iota(jnp.int32, (tk, tq), 1)
    return kpos, qpos


def _walk_key_blocks(step, q_start, tk):
    kb_diag = q_start // tk
    go = step(kb_diag, True)

    def cond(carry):
        kb, go = carry
        return jnp.logical_and(kb >= 0, go)

    def body(carry):
        kb, _ = carry
        return kb - 1, step(kb, False)

    lax.while_loop(cond, body, (kb_diag - 1, go))


def _pick_row(block, idx):
    row = lax.broadcasted_iota(jnp.int32, block.shape, 0)
    return jnp.sum(jnp.where(row == idx, block, 0.0), axis=0, keepdims=True)


def _gate_rows(f_row, key_side):
    n = f_row.shape[1]
    parts = [p.astype(F32) for p in _split3(-f_row if key_side else f_row)]
    ones = [jnp.ones((1, n), F32)] * 3
    row = lax.broadcasted_iota(jnp.int32, (SUBLANES, n), 0)
    out = jnp.zeros((SUBLANES, n), F32)
    for i, v in enumerate(parts + ones if key_side else ones + parts):
        out = jnp.where(row == i, v, out)
    return out


def _with_gate_rows(x_t, gates, head):
    n = x_t.shape[1]
    fill = jnp.zeros((HEAD_DIM - SUBLANES, n), F32)
    if head == 0:
        return jnp.concatenate([x_t[:HEAD_DIM], gates, fill], axis=0)
    return jnp.concatenate([gates, fill, x_t[HEAD_DIM:]], axis=0)


def _fox_kernel(*refs, tq, tk, off, n_heads, chunks):
    n_src = 1 + max(c[0] for c in chunks)
    bound_ref, fend_ref, q_ref = refs[:3]
    kv_refs = refs[3:3 + 2 * n_src]
    f_ref, o_ref, k_scr, vt_scr, m_scr, l_scr, acc_scr = refs[3 + 2 * n_src:]
    b = pl.program_id(0)
    group = pl.program_id(1)
    qi = pl.program_id(2)
    n_kb = vt_scr.shape[1] // tk
    n_pairs = vt_scr.shape[0] // LANES
    first_head = 2 * n_pairs * group

    @pl.when(qi == 0)
    def _():
        filled = max(dst + w for _, _, w, dst in chunks)
        if filled < n_kb * tk:
            k_scr[:, filled:, :] = jnp.zeros((2 * n_pairs, n_kb * tk - filled, LANES), BF16)
            vt_scr[:, filled:] = jnp.zeros((n_pairs * LANES, n_kb * tk - filled), BF16)
        for src, start, w, dst in chunks:
            k_ref, v_ref = kv_refs[2 * src], kv_refs[2 * src + 1]
            vt_scr[:, dst:dst + w] = v_ref[0, 0, :, start:start + w].astype(BF16)
            for p in range(n_pairs):
                k_t = k_ref[0, 0, p * LANES:(p + 1) * LANES, start:start + w]
                for j in range(2):
                    gates = _gate_rows(_pick_row(f_ref[0, :, dst:dst + w], first_head + 2 * p + j), True)
                    k_scr[2 * p + j, dst:dst + w, :] = _with_gate_rows(k_t, gates, j).T.astype(BF16)

    q_start = off + qi * tq
    qs, fqs = [], []
    for p in range(n_pairs):
        q_t = q_ref[0, p * LANES:(p + 1) * LANES, :].astype(F32)
        for j in range(2):
            fq = _pick_row(f_ref[0, :, pl.ds(pl.multiple_of(q_start, LANES), tq)], first_head + 2 * p + j)
            fqs.append(fq)
            qs.append(_with_gate_rows(q_t, _gate_rows(fq, False), j).astype(BF16))

    m_scr[...] = jnp.full_like(m_scr, NEG)
    l_scr[...] = jnp.zeros_like(l_scr)
    acc_scr[...] = jnp.zeros_like(acc_scr)

    def step(kb, masked):
        ks = pl.multiple_of(kb * tk, tk)
        if masked:
            kpos, qpos = _positions(q_start, ks, tq, tk)
            visible = kpos <= qpos
        heads = range(2 * n_pairs)
        scores = [_dot(k_scr[h, pl.ds(ks, tk), :], qs[h]) for h in heads]
        go = False
        alphas, probs = [], []
        for h in heads:
            s = scores[h]
            if masked:
                s = jnp.where(visible, s, NEG)
            m_prev = m_scr[h]
            m_new = jnp.maximum(m_prev, jnp.max(s, axis=0, keepdims=True))
            alpha = jnp.exp2(m_prev - m_new)
            p = jnp.exp2(s - m_new)
            l_scr[h] = alpha * l_scr[h] + jnp.sum(p, axis=0, keepdims=True)
            m_scr[h] = m_new
            alphas.append(alpha)
            probs.append(p.astype(BF16))
            f_end = fend_ref[(b * n_heads + first_head + h) * n_kb + jnp.maximum(kb - 1, 0)]
            reach = jnp.max(fqs[h] - m_new) + bound_ref[0] - f_end
            go = jnp.logical_or(go, reach >= EXP2_ZERO_BELOW)
        pvs = [_dot(vt_scr[h * HEAD_DIM:(h + 1) * HEAD_DIM, pl.ds(ks, tk)], probs[h]) for h in heads]
        for h in heads:
            acc_scr[h] = acc_scr[h] * alphas[h] + pvs[h]
        return go

    _walk_key_blocks(step, q_start, tk)
    o_t = jnp.concatenate([acc_scr[h] / l_scr[h] for h in range(2 * n_pairs)], axis=0)
    o_ref[0] = o_t.T


def _sb_kernel(*refs, tq, tk, off, chunks):
    n_src = 1 + max(c[0] for c in chunks)
    q_ref = refs[0]
    kv_refs = refs[1:1 + 2 * n_src]
    tri_ref, o_ref, kb_scr, vt_scr, c_scr, acc_scr = refs[1 + 2 * n_src:]
    qi = pl.program_id(2)
    n_kb = vt_scr.shape[1] // tk
    n_pairs = vt_scr.shape[0] // LANES
    row_lo = lax.broadcasted_iota(jnp.int32, (LANES, 1), 0) < HEAD_DIM

    @pl.when(qi == 0)
    def _():
        filled = max(dst + w for _, _, w, dst in chunks)
        if filled < n_kb * tk:
            kb_scr[filled:, :] = jnp.zeros((n_kb * tk - filled, n_pairs * LANES), BF16)
            vt_scr[:, filled:] = jnp.zeros((n_pairs * LANES, n_kb * tk - filled), BF16)
        for src, start, w, dst in chunks:
            k_ref, v_ref = kv_refs[2 * src], kv_refs[2 * src + 1]
            vt_scr[:, dst:dst + w] = v_ref[0, 0, :, start:start + w].astype(BF16)
            for p in range(n_pairs):
                lanes = slice(p * LANES, (p + 1) * LANES)
                kb_scr[dst:dst + w, lanes] = k_ref[0, 0, lanes, start:start + w].T.astype(BF16)

    qs = []
    for p in range(n_pairs):
        q_t = q_ref[0, p * LANES:(p + 1) * LANES, :]
        zero = jnp.zeros_like(q_t)
        qs += [jnp.where(row_lo, q_t, zero), jnp.where(row_lo, zero, q_t)]
    c_scr[...] = jnp.zeros_like(c_scr)
    acc_scr[...] = jnp.zeros_like(acc_scr)
    q_start = off + qi * tq

    def step(kb, masked):
        ks = pl.multiple_of(kb * tk, tk)
        if masked:
            kpos, qpos = _positions(q_start, ks, tq, tk)
            strict = kpos < qpos
        heads = range(2 * n_pairs)
        zs = [_dot(kb_scr[pl.ds(ks, tk), (h // 2) * LANES:(h // 2 + 1) * LANES], qs[h])
              for h in heads]
        log_bs, log_1ms, c_olds = [], [], []
        reach = None
        for h in heads:
            z = zs[h]
            log_b = jnp.minimum(z, 0.0) - jnp.log2(1.0 + jnp.exp2(-jnp.abs(z)))
            log_1m = log_b - z
            if masked:
                log_1m = jnp.where(strict, log_1m, 0.0)
            c_old = c_scr[h]
            c_new = c_old + jnp.sum(log_1m, axis=0, keepdims=True)
            c_scr[h] = c_new
            reach = c_new if reach is None else jnp.maximum(reach, c_new)
            log_bs.append(log_b)
            log_1ms.append(log_1m.astype(BF16))
            c_olds.append(c_old)
        rests = [_dot(tri_ref[...], log_1ms[h]) for h in heads]
        weights = []
        for h in heads:
            arg = log_bs[h] + (rests[h] + c_olds[h])
            if masked:
                arg = jnp.where(strict, arg, NEG)
            weights.append(jnp.exp2(arg).astype(BF16))
        pvs = [_dot(vt_scr[h * HEAD_DIM:(h + 1) * HEAD_DIM, pl.ds(ks, tk)], weights[h]) for h in heads]
        for h in heads:
            acc_scr[h] = acc_scr[h] + pvs[h]
        return jnp.max(reach) >= EXP2_ZERO_BELOW

    _walk_key_blocks(step, q_start, tk)
    o_ref[0] = jnp.concatenate([acc_scr[h] for h in range(2 * n_pairs)], axis=0).T


def _attn_geometry(q_t, new, past, layer):
    b, width, tq_total = q_t.shape
    n_new = new[0].shape[3]
    n_past = 0 if past is None else past[0].shape[3]
    tk = KEY_BLOCK
    tq = 256 if tq_total % 256 == 0 else LANES
    assert tq_total % tq == 0 and tk % tq == 0 and n_past % tq == 0 and n_new >= tq_total
    chunks = _key_chunks(n_past, n_new, tk)
    tk_total = -(-(n_past + n_new) // tk) * tk
    gw = PAIRS_PER_STEP * LANES
    grid = (b, width // gw, tq_total // tq)
    q_spec = pl.BlockSpec((1, gw, tq), lambda i, g, j: (i, g, j))
    o_spec = pl.BlockSpec((1, tq, gw), lambda i, g, j: (i, j, g))
    kv_ins, kv_specs = [], []
    for src in ([past] if past is not None else []) + [new]:
        n = src[0].shape[3]
        kv_ins += list(src)
        kv_specs += [pl.BlockSpec((1, 1, gw, n), lambda i, g, j: (layer, i, g, 0),
                                  pipeline_mode=pl.Buffered(1))] * 2
    return dict(b=b, width=width, tq_total=tq_total, tq=tq, tk=tk, off=n_past, chunks=chunks,
                tk_total=tk_total, gw=gw, grid=grid, q_spec=q_spec, o_spec=o_spec,
                kv_ins=kv_ins, kv_specs=kv_specs)


def _fox_attention(q_t, new, past, layer, f2, qk_bound):
    G = _attn_geometry(q_t, new, past, layer)
    b, tq, tk, tk_total, gw = G["b"], G["tq"], G["tk"], G["tk_total"], G["gw"]
    n_heads = G["width"] // HEAD_DIM
    assert f2.shape == (b, n_heads, tk_total)
    f_end = f2[:, :, tk - 1::tk].reshape(-1)
    heads = gw // HEAD_DIM
    return pl.pallas_call(
        functools.partial(_fox_kernel, tq=tq, tk=tk, off=G["off"], n_heads=n_heads, chunks=G["chunks"]),
        grid=G["grid"],
        in_specs=[_SMEM, _SMEM, G["q_spec"]] + G["kv_specs"]
                 + [pl.BlockSpec((1, n_heads, tk_total), lambda i, g, j: (i, 0, 0))],
        out_specs=G["o_spec"],
        out_shape=jax.ShapeDtypeStruct((b, G["tq_total"], G["width"]), F32),
        scratch_shapes=[pltpu.VMEM((heads, tk_total, LANES), BF16),
                        pltpu.VMEM((gw, tk_total), BF16),
                        pltpu.VMEM((heads, 1, tq), F32), pltpu.VMEM((heads, 1, tq), F32),
                        pltpu.VMEM((heads, HEAD_DIM, tq), F32)],
        compiler_params=_params(3),
        name="fox_attention",
    )(qk_bound, f_end, q_t, *G["kv_ins"], f2)


def _sb_attention(q_t, new, past, layer):
    G = _attn_geometry(q_t, new, past, layer)
    b, tq, tk, tk_total, gw = G["b"], G["tq"], G["tk"], G["tk_total"], G["gw"]
    r = lax.broadcasted_iota(jnp.int32, (tk, tk), 0)
    c = lax.broadcasted_iota(jnp.int32, (tk, tk), 1)
    tri = (c > r).astype(BF16)
    heads = gw // HEAD_DIM
    return pl.pallas_call(
        functools.partial(_sb_kernel, tq=tq, tk=tk, off=G["off"], chunks=G["chunks"]),
        grid=G["grid"],
        in_specs=[G["q_spec"]] + G["kv_specs"] + [_resident(tri)],
        out_specs=G["o_spec"],
        out_shape=jax.ShapeDtypeStruct((b, G["tq_total"], G["width"]), F32),
        scratch_shapes=[pltpu.VMEM((tk_total, gw), BF16), pltpu.VMEM((gw, tk_total), BF16),
                        pltpu.VMEM((heads, 1, tq), F32), pltpu.VMEM((heads, HEAD_DIM, tq), F32)],
        compiler_params=_params(3),
        name="sb_attention",
    )(q_t, *G["kv_ins"], tri)


def _mix_out_cross_kernel(of_ref, os_ref, x_ref, mk_ref, mv_ref, gf_ref, gs_ref, w_ref,
                          g_ref, wq_ref, gq_ref, wo_ref, y_ref):
    a = _rms(of_ref[0], gf_ref[...]).astype(BF16)
    b = _rms(os_ref[0], gs_ref[...]).astype(BF16)
    nf = a.shape[1]
    x = x_ref[0] + _dot(a, w_ref[0:nf, :]) + _dot(b, w_ref[nf:, :])
    h = _rms(x, g_ref[...]).astype(BF16)
    hd = wq_ref.shape[1] // MEM_HEADS
    scale = hd ** -0.5
    heads = [slice(i * hd, (i + 1) * hd) for i in range(MEM_HEADS)]
    q_all = _dot(h, wq_ref[...])
    qs = [(_rms(q_all[:, sl], gq_ref[...]) * scale).astype(BF16) for sl in heads]
    scores = [_dot_nt(q, mk_ref[0, :, sl].astype(BF16)) for q, sl in zip(qs, heads)]
    es = [jnp.exp(s - jnp.max(s, axis=1, keepdims=True)) for s in scores]
    pvs = [_dot(e.astype(BF16), mv_ref[0, :, sl].astype(BF16)) for e, sl in zip(es, heads)]
    o = jnp.concatenate([(pv / jnp.sum(e, axis=1, keepdims=True)).astype(BF16)
                         for pv, e in zip(pvs, es)], axis=1)
    y_ref[0] = x + _dot(o, wo_ref[...])


def _mix_out_cross(o_f, o_s, x, mem_k, mem_v, P):
    b, t, d = x.shape
    tm = min(512, t)
    row = lambda w: pl.BlockSpec((1, tm, w), lambda i, j: (i, j, 0))
    mem = pl.BlockSpec((1,) + mem_k.shape[1:], lambda i, j: (i, 0, 0))
    ins = [o_f, o_s, x, mem_k, mem_v, P["g_out_fox"], P["g_out_sb"], P["w_out"],
           P["g_cross"], P["w_mq"], P["g_mq"], P["w_mo"]]
    return pl.pallas_call(
        _mix_out_cross_kernel,
        grid=(b, t // tm),
        in_specs=[row(o_f.shape[2]), row(o_s.shape[2]), row(d), mem, mem] + [_resident(a) for a in ins[5:]],
        out_specs=row(d),
        out_shape=jax.ShapeDtypeStruct((b, t, d), F32),
        compiler_params=_params(2),
        name="mix_out_cross",
    )(*ins)


def _ffn_kernel(x_ref, g_ref, w1_ref, w2_ref, y_ref, *, chunk):
    x = x_ref[...]
    h = _rms(x, g_ref[...]).astype(BF16)
    acc = x
    for c in range(w1_ref.shape[1] // chunk):
        sl = slice(c * chunk, (c + 1) * chunk)
        u = jnp.square(jnp.maximum(_dot(h, w1_ref[:, sl]), 0.0)).astype(BF16)
        acc = acc + _dot(u, w2_ref[sl, :])
    y_ref[...] = acc


def _ffn(x2d, P):
    n, d = x2d.shape
    tm = min(512, n)
    row = pl.BlockSpec((tm, d), lambda i: (i, 0))
    once = lambda a: pl.BlockSpec(a.shape, lambda i: (0, 0), pipeline_mode=pl.Buffered(1))
    return pl.pallas_call(
        functools.partial(_ffn_kernel, chunk=1024),
        grid=(n // tm,),
        in_specs=[row, _resident(P["g_ffn"]), once(P["w_ff1"]), once(P["w_ff2"])],
        out_specs=row,
        out_shape=jax.ShapeDtypeStruct((n, d), F32),
        compiler_params=_params(1),
        name="ffn",
    )(x2d, P["g_ffn"], P["w_ff1"], P["w_ff2"])


def _mem_kv_kernel(m_ref, g_ref, wk_ref, wv_ref, gk_ref, k_ref, v_ref):
    h = _rms(m_ref[...], g_ref[...]).astype(BF16)
    hd = wk_ref.shape[1] // MEM_HEADS
    for i in range(MEM_HEADS):
        sl = slice(i * hd, (i + 1) * hd)
        k_ref[:, sl] = _rms(_dot(h, wk_ref[:, sl]), gk_ref[...])
    v_ref[...] = _dot(h, wv_ref[...])


def _memory_kv(mem2d, P):
    n, d = mem2d.shape
    w = P["w_mk"].shape[1]
    tm = min(256, n)
    row = lambda c: pl.BlockSpec((tm, c), lambda i: (i, 0))
    ins = [mem2d, P["g_mem"], P["w_mk"], P["w_mv"], P["g_mk"]]
    return pl.pallas_call(
        _mem_kv_kernel,
        grid=(n // tm,),
        in_specs=[row(d)] + [_resident(a) for a in ins[1:]],
        out_specs=[row(w), row(w)],
        out_shape=[jax.ShapeDtypeStruct((n, w), F32)] * 2,
        compiler_params=_params(1),
        name="memory_kv",
    )(*ins)


def _layer(x, mem_k, mem_v, past, P, layer, depth, stacks):
    b, t, d = x.shape
    qf_t, qs_t, stacks = _proj_in(x, P, layer, depth, stacks)
    kf, vf, ks, vs, lf = stacks
    lf_t = lf[layer]
    if past is not None:
        lf_t = jnp.concatenate([past[2][layer], lf_t], axis=2)
    pad = -lf_t.shape[2] % KEY_BLOCK
    f2 = _cumsum_logf(jnp.pad(lf_t, ((0, 0), (0, 0), (0, pad))))
    o_f = _fox_attention(qf_t, (kf, vf), None if past is None else (past[0], past[1]), layer, f2, P["qk_bound"])
    o_s = _sb_attention(qs_t, (ks, vs), None if past is None else (past[3], past[4]), layer)
    y = _mix_out_cross(o_f, o_s, x, mem_k, mem_v, P)
    y = _ffn(y.reshape(b * t, d), P)
    return y.reshape(b, t, d), stacks


def _layer_params(l, g_mix, w_in, b_forget, g_fox_q, g_fox_k, g_out_fox, g_out_sb, w_out,
                  g_cross, g_mem, w_mq, w_mk, w_mv, g_mq, g_mk, w_mo, g_ffn, w_ff1, w_ff2):
    fox_w = g_out_fox.shape[1]
    sb_w = g_out_sb.shape[1]
    n_heads = b_forget.shape[1]
    assert fox_w == sb_w and n_heads == SUBLANES and g_fox_q.shape[1] == HEAD_DIM
    w = w_in[l]
    o_sb = 3 * fox_w + n_heads
    w_t = jnp.concatenate(
        [w[:, :3 * fox_w], w[:, o_sb:o_sb + 3 * sb_w],
         jnp.pad(w[:, 3 * fox_w:o_sb], ((0, 0), (0, 2 * SUBLANES - n_heads)))], axis=1).T.astype(BF16)
    row = lambda a: a[l].reshape(1, -1).astype(F32)
    col = lambda a: a[l].reshape(-1, 1).astype(F32)
    qk_bound = (LOG2E * HEAD_DIM ** 0.5 * 1.01) * jnp.max(jnp.abs(g_fox_q[l])) * jnp.max(jnp.abs(g_fox_k[l]))
    return dict(
        qk_bound=qk_bound.reshape(1).astype(F32),
        g_mix=row(g_mix), w_in_t=w_t,
        gq=jnp.tile(col(g_fox_q), (fox_w // HEAD_DIM, 1)),
        gk=jnp.tile(col(g_fox_k), (fox_w // HEAD_DIM, 1)),
        b_forget=col(b_forget),
        g_out_fox=row(g_out_fox), g_out_sb=row(g_out_sb), w_out=w_out[l].astype(BF16),
        g_cross=row(g_cross), w_mq=w_mq[l].astype(BF16), g_mq=row(g_mq), w_mo=w_mo[l].astype(BF16),
        g_mem=row(g_mem), w_mk=w_mk[l].astype(BF16), w_mv=w_mv[l].astype(BF16), g_mk=row(g_mk),
        g_ffn=row(g_ffn), w_ff1=w_ff1[l].astype(BF16), w_ff2=w_ff2[l].astype(BF16))


def _feature_major(cache):
    depth, b, t = cache.shape[:3]
    return jnp.moveaxis(cache.reshape(depth, b, t, -1), 2, 3)


def _time_major(stack, t, hd):
    depth, b, w, _ = stack.shape
    return jnp.moveaxis(stack[..., :t].reshape(depth, b, w // hd, hd, t), 4, 2)


def kernel(x_prompt, x_sample, mem_prompt, cache_fox_k, cache_fox_v, cache_fox_logf, cache_sb_k, cache_sb_v, cache_mem_k, cache_mem_v, g_mix, w_in, b_forget, g_fox_q, g_fox_k, g_out_fox, g_out_sb, w_out, g_cross, g_mem, w_mq, w_mk, w_mv, g_mq, g_mk, w_mo, g_ffn, w_ff1, w_ff2):
    depth = g_mix.shape[0]
    bp, tp, d = x_prompt.shape
    bs, ts, _ = x_sample.shape
    n_mem = mem_prompt.shape[1]
    n_heads = b_forget.shape[1]
    ts_pad = -(-ts // LANES) * LANES
    xp = x_prompt
    xs = jnp.pad(x_sample, ((0, 0), (0, ts_pad - ts), (0, 0)))
    mem2d = mem_prompt.reshape(bp * n_mem, d)
    past = (_feature_major(cache_fox_k), _feature_major(cache_fox_v),
            jnp.moveaxis(cache_fox_logf.astype(F32), 2, 3),
            _feature_major(cache_sb_k), _feature_major(cache_sb_v))
    stacks_p = stacks_s = None
    p_mk, p_mv = [], []
    for l in range(depth):
        P = _layer_params(l, g_mix, w_in, b_forget, g_fox_q, g_fox_k, g_out_fox, g_out_sb, w_out,
                          g_cross, g_mem, w_mq, w_mk, w_mv, g_mq, g_mk, w_mo, g_ffn, w_ff1, w_ff2)
        mk, mv = _memory_kv(mem2d, P)
        mk = mk.reshape(bp, n_mem, -1)
        mv = mv.reshape(bp, n_mem, -1)
        p_mk.append(mk)
        p_mv.append(mv)
        xp, stacks_p = _layer(xp, mk, mv, None, P, l, depth, stacks_p)
        xs, stacks_s = _layer(xs, cache_mem_k[l].reshape(bs, n_mem, -1), cache_mem_v[l].reshape(bs, n_mem, -1),
                              past, P, l, depth, stacks_s)

    mem_hd = p_mk[0].shape[-1] // MEM_HEADS
    p_fk, p_fv, p_sk, p_sv, p_lf = stacks_p
    s_fk, s_fv, s_sk, s_sv, s_lf = stacks_s
    return (xp, xs[:, :ts],
            _time_major(p_fk, tp, HEAD_DIM), _time_major(p_fv, tp, HEAD_DIM),
            jnp.moveaxis(p_lf, 2, 3),
            _time_major(p_sk, tp, HEAD_DIM), _time_major(p_sv, tp, HEAD_DIM),
            jnp.stack(p_mk).reshape(depth, bp, n_mem, MEM_HEADS, mem_hd),
            jnp.stack(p_mv).reshape(depth, bp, n_mem, MEM_HEADS, mem_hd),
            _time_major(s_fk, ts, HEAD_DIM), _time_major(s_fv, ts, HEAD_DIM),
            jnp.moveaxis(s_lf[..., :ts], 2, 3),
            _time_major(s_sk, ts, HEAD_DIM), _time_major(s_sv, ts, HEAD_DIM))
```

```python
import functools
import math

import jax
import jax.numpy as jnp
from jax import lax
from jax.experimental import pallas as pl
from jax.experimental.pallas import tpu as pltpu

EPS = 1e-6
NEG = -1e30
HEAD_DIM = 64
MEM_HEADS = 4
LANES = 128
SUBLANES = 8
VMEM_LIMIT_BYTES = 56 * 2**20
LOG2E = math.log2(math.e)
EXP2_ZERO_BELOW = -106.0 * LOG2E
KEY_BLOCK = 256
PAIRS_PER_STEP = 4
F32 = jnp.float32
BF16 = jnp.bfloat16


def _dot(a, b):
    return jnp.dot(a, b, preferred_element_type=F32)


def _dot_nt(a, b):
    return lax.dot_general(a, b, (((1,), (1,)), ((), ())), preferred_element_type=F32)


def _rms(x, g):
    return x * lax.rsqrt(jnp.mean(x * x, axis=-1, keepdims=True) + EPS) * g


def _split3(x):
    hi = x.astype(BF16)
    r = x - hi.astype(F32)
    mid = r.astype(BF16)
    lo = (r - mid.astype(F32)).astype(BF16)
    return hi, mid, lo


def _params(n_grid_dims):
    return pltpu.CompilerParams(
        dimension_semantics=("arbitrary",) * n_grid_dims,
        vmem_limit_bytes=VMEM_LIMIT_BYTES)


def _resident(arr):
    nd = arr.ndim
    return pl.BlockSpec(arr.shape, lambda *_: (0,) * nd)


_SMEM = pl.BlockSpec(memory_space=pltpu.SMEM)
_ANY = pl.BlockSpec(memory_space=pl.ANY)


def _proj_in_kernel(*refs, width, n_prev):
    x_ref, g_ref, wt_ref, gq_ref, gk_ref, bf_ref = refs[:6]
    qf_ref, qs_ref, kf_ref, vf_ref, ks_ref, vs_ref, lf_ref = refs[6 + n_prev:]
    ht = _rms(x_ref[0], g_ref[...]).T.astype(BF16)
    q_scale = LOG2E * HEAD_DIM ** -0.5

    def seg(i, rows=width):
        return _dot(wt_ref[i * width:i * width + rows, :], ht)

    def head_norm(y, g_col, store):
        for h in range(width // HEAD_DIM):
            rows = slice(h * HEAD_DIM, (h + 1) * HEAD_DIM)
            yh = y[rows, :]
            ms = jnp.mean(yh * yh, axis=0, keepdims=True)
            store(rows, yh * lax.rsqrt(ms + EPS) * g_col[rows, :])

    def store_qf(rows, y):
        qf_ref[0, rows, :] = (y * q_scale).astype(BF16)

    def store_kf(rows, y):
        kf_ref[0, 0, rows, :] = y

    head_norm(seg(0), gq_ref[...], store_qf)
    head_norm(seg(1), gk_ref[...], store_kf)
    vf_ref[0, 0] = seg(2)
    qs_ref[0] = (seg(3) * q_scale).astype(BF16)
    ks_ref[0, 0] = seg(4)
    vs_ref[0, 0] = seg(5)
    n_gates = lf_ref.shape[2]
    lf_ref[0, 0] = jax.nn.log_sigmoid(seg(6, 2 * SUBLANES)[0:n_gates, :] + bf_ref[...])


def _proj_in(x, P, layer, depth, prev):
    b, t, d = x.shape
    width = P["gq"].shape[0]
    n_gates = P["b_forget"].shape[0]
    tm = min(512, t)
    n_prev = 0 if prev is None else len(prev)
    ins = [x, P["g_mix"], P["w_in_t"], P["gq"], P["gk"], P["b_forget"]]
    q_spec = pl.BlockSpec((1, width, tm), lambda i, j: (i, 0, j))
    kv_spec = pl.BlockSpec((1, 1, width, tm), lambda i, j: (layer, i, 0, j))
    lf_spec = pl.BlockSpec((1, 1, n_gates, tm), lambda i, j: (layer, i, 0, j))
    sds = jax.ShapeDtypeStruct
    kv_shape = sds((depth, b, width, t), F32)
    outs = pl.pallas_call(
        functools.partial(_proj_in_kernel, width=width, n_prev=n_prev),
        grid=(b, t // tm),
        in_specs=[pl.BlockSpec((1, tm, d), lambda i, j: (i, j, 0))]
                 + [_resident(a) for a in ins[1:]] + [_ANY] * n_prev,
        out_specs=[q_spec, q_spec] + [kv_spec] * 4 + [lf_spec],
        out_shape=[sds((b, width, t), BF16)] * 2 + [kv_shape] * 4 + [sds((depth, b, n_gates, t), F32)],
        input_output_aliases={len(ins) + i: 2 + i for i in range(n_prev)},
        compiler_params=_params(2),
        name="proj_in",
    )(*ins, *(prev or ()))
    return outs[0], outs[1], tuple(outs[2:])


def _cumsum_kernel(lf_ref, tri_ref, f_ref, carry_ref):
    @pl.when(pl.program_id(0) == 0)
    def _():
        carry_ref[...] = jnp.zeros_like(carry_ref)

    tri = tri_ref[...]
    f = sum(_dot(part, tri) for part in _split3(lf_ref[...])) + carry_ref[...]
    f_ref[...] = f * LOG2E
    tb = f.shape[1]
    carry_ref[...] = f[:, tb - 1:tb]


def _cumsum_logf(lf_t):
    b, n_gates, t = lf_t.shape
    rows = b * n_gates
    tb = 512 if t % 512 == 0 else KEY_BLOCK
    r = lax.broadcasted_iota(jnp.int32, (tb, tb), 0)
    c = lax.broadcasted_iota(jnp.int32, (tb, tb), 1)
    tri = (r <= c).astype(BF16)
    spec = pl.BlockSpec((rows, tb), lambda j: (0, j))
    return pl.pallas_call(
        _cumsum_kernel,
        grid=(t // tb,),
        in_specs=[spec, _resident(tri)],
        out_specs=spec,
        out_shape=jax.ShapeDtypeStruct((rows, t), F32),
        scratch_shapes=[pltpu.VMEM((rows, 1), F32)],
        compiler_params=_params(1),
        name="cumsum_logf",
    )(lf_t.reshape(rows, t), tri).reshape(b, n_gates, t)


def _key_chunks(n_past, n_new, tk):
    assert n_past % tk == 0
    new_src = 1 if n_past else 0
    chunks = [(0, s, tk, s) for s in range(0, n_past, tk)]
    chunks += [(new_src, s, min(tk, n_new - s), n_past + s) for s in range(0, n_new, tk)]
    return chunks


def _positions(q_start, ks, tq, tk):
    kpos = ks + lax.broadcasted_iota(jnp.int32, (tk, tq), 0)
    qpos = q_start + lax.broadcasted_iota(jnp.int32, (tk, tq), 1)
    return kpos, qpos


def _walk_key_blocks(step, q_start, tk):
    kb_diag = q_start // tk
    go = step(kb_diag, True)

    def cond(carry):
        kb, go = carry
        return jnp.logical_and(kb >= 0, go)

    def body(carry):
        kb, _ = carry
        return kb - 1, step(kb, False)

    lax.while_loop(cond, body, (kb_diag - 1, go))


def _stage_lazily(stage, stage_static, chunks, off, tk, ks, masked):
    if masked:
        stage_static([c for c in chunks if c[3] >= off], off, tk)
    else:
        stage(0, ks, tk, 0, ks)
    return 0


def _pick_row(block, idx):
    row = lax.broadcasted_iota(jnp.int32, block.shape, 0)
    return jnp.sum(jnp.where(row == idx, block, 0.0), axis=0, keepdims=True)


def _gate_rows(f_row, key_side):
    n = f_row.shape[1]
    parts = [p.astype(F32) for p in _split3(-f_row if key_side else f_row)]
    ones = [jnp.ones((1, n), F32)] * 3
    row = lax.broadcasted_iota(jnp.int32, (SUBLANES, n), 0)
    out = jnp.zeros((SUBLANES, n), F32)
    for i, v in enumerate(parts + ones if key_side else ones + parts):
        out = jnp.where(row == i, v, out)
    return out


def _with_gate_rows(x_t, gates, head):
    n = x_t.shape[1]
    fill = jnp.zeros((HEAD_DIM - SUBLANES, n), F32)
    if head == 0:
        return jnp.concatenate([x_t[:HEAD_DIM], gates, fill], axis=0)
    return jnp.concatenate([gates, fill, x_t[HEAD_DIM:]], axis=0)


def _fox_kernel(*refs, tq, tk, off, n_heads, chunks, lazy):
    n_src = 1 + max(c[0] for c in chunks)
    bound_ref, fend_ref, q_ref = refs[:3]
    kv_refs = refs[3:3 + 2 * n_src]
    f_ref, o_ref, k_scr, vt_scr, m_scr, l_scr, acc_scr = refs[3 + 2 * n_src:]
    b = pl.program_id(0)
    group = pl.program_id(1)
    qi = pl.program_id(2)
    n_kb = f_ref.shape[2] // tk
    n_pairs = vt_scr.shape[0] // LANES
    first_head = 2 * n_pairs * group

    def stage(src, start, w, at, pos):
        k_ref, v_ref = kv_refs[2 * src], kv_refs[2 * src + 1]
        vt_scr[:, at:at + w] = v_ref[0, 0, :, pl.ds(start, w)].astype(BF16)
        f_blk = f_ref[0, :, pl.ds(pos, w)]
        for p in range(n_pairs):
            k_t = k_ref[0, 0, p * LANES:(p + 1) * LANES, pl.ds(start, w)]
            for j in range(2):
                gates = _gate_rows(_pick_row(f_blk, first_head + 2 * p + j), True)
                k_scr[2 * p + j, at:at + w, :] = _with_gate_rows(k_t, gates, j).T.astype(BF16)

    def stage_static(pieces, base, size):
        filled = max(dst + w for _, _, w, dst in pieces) - base
        if filled < size:
            k_scr[:, filled:size, :] = jnp.zeros((2 * n_pairs, size - filled, LANES), BF16)
            vt_scr[:, filled:size] = jnp.zeros((n_pairs * LANES, size - filled), BF16)
        for src, start, w, dst in pieces:
            stage(src, start, w, dst - base, dst)

    if not lazy:
        @pl.when(qi == 0)
        def _():
            stage_static(chunks, 0, n_kb * tk)

    q_start = off + qi * tq
    qs, fqs = [], []
    for p in range(n_pairs):
        q_t = q_ref[0, p * LANES:(p + 1) * LANES, :].astype(F32)
        for j in range(2):
            fq = _pick_row(f_ref[0, :, pl.ds(pl.multiple_of(q_start, LANES), tq)], first_head + 2 * p + j)
            fqs.append(fq)
            qs.append(_with_gate_rows(q_t, _gate_rows(fq, False), j).astype(BF16))

    m_scr[...] = jnp.full_like(m_scr, NEG)
    l_scr[...] = jnp.zeros_like(l_scr)
    acc_scr[...] = jnp.zeros_like(acc_scr)

    def step(kb, masked):
        ks = pl.multiple_of(kb * tk, tk)
        if masked:
            kpos, qpos = _positions(q_start, ks, tq, tk)
            visible = kpos <= qpos
        at = _stage_lazily(stage, stage_static, chunks, off, tk, ks, masked) if lazy else ks
        heads = range(2 * n_pairs)
        scores = [_dot(k_scr[h, pl.ds(at, tk), :], qs[h]) for h in heads]
        go = False
        alphas, probs = [], []
        for h in heads:
            s = scores[h]
            if masked:
                s = jnp.where(visible, s, NEG)
            m_prev = m_scr[h]
            m_new = jnp.maximum(m_prev, jnp.max(s, axis=0, keepdims=True))
            alpha = jnp.exp2(m_prev - m_new)
            p = jnp.exp2(s - m_new)
            l_scr[h] = alpha * l_scr[h] + jnp.sum(p, axis=0, keepdims=True)
            m_scr[h] = m_new
            alphas.append(alpha)
            probs.append(p.astype(BF16))
            f_end = fend_ref[(b * n_heads + first_head + h) * n_kb + jnp.maximum(kb - 1, 0)]
            reach = jnp.max(fqs[h] - m_new) + bound_ref[0] - f_end
            go = jnp.logical_or(go, reach >= EXP2_ZERO_BELOW)
        pvs = [_dot(vt_scr[h * HEAD_DIM:(h + 1) * HEAD_DIM, pl.ds(at, tk)], probs[h]) for h in heads]
        for h in heads:
            acc_scr[h] = acc_scr[h] * alphas[h] + pvs[h]
        return go

    _walk_key_blocks(step, q_start, tk)
    o_t = jnp.concatenate([acc_scr[h] / l_scr[h] for h in range(2 * n_pairs)], axis=0)
    o_ref[0] = o_t.T


def _sb_kernel(*refs, tq, tk, off, chunks, lazy):
    n_src = 1 + max(c[0] for c in chunks)
    q_ref = refs[0]
    kv_refs = refs[1:1 + 2 * n_src]
    tri_ref, o_ref, kb_scr, vt_scr, c_scr, acc_scr = refs[1 + 2 * n_src:]
    qi = pl.program_id(2)
    n_pairs = vt_scr.shape[0] // LANES
    row_lo = lax.broadcasted_iota(jnp.int32, (LANES, 1), 0) < HEAD_DIM

    def stage(src, start, w, at, pos):
        del pos
        k_ref, v_ref = kv_refs[2 * src], kv_refs[2 * src + 1]
        vt_scr[:, at:at + w] = v_ref[0, 0, :, pl.ds(start, w)].astype(BF16)
        for p in range(n_pairs):
            lanes = slice(p * LANES, (p + 1) * LANES)
            kb_scr[at:at + w, lanes] = k_ref[0, 0, lanes, pl.ds(start, w)].T.astype(BF16)

    def stage_static(pieces, base, size):
        filled = max(dst + w for _, _, w, dst in pieces) - base
        if filled < size:
            kb_scr[filled:size, :] = jnp.zeros((size - filled, n_pairs * LANES), BF16)
            vt_scr[:, filled:size] = jnp.zeros((n_pairs * LANES, size - filled), BF16)
        for src, start, w, dst in pieces:
            stage(src, start, w, dst - base, dst)

    if not lazy:
        @pl.when(qi == 0)
        def _():
            stage_static(chunks, 0, vt_scr.shape[1])

    qs = []
    for p in range(n_pairs):
        q_t = q_ref[0, p * LANES:(p + 1) * LANES, :]
        zero = jnp.zeros_like(q_t)
        qs += [jnp.where(row_lo, q_t, zero), jnp.where(row_lo, zero, q_t)]
    c_scr[...] = jnp.zeros_like(c_scr)
    acc_scr[...] = jnp.zeros_like(acc_scr)
    q_start = off + qi * tq

    def step(kb, masked):
        ks = pl.multiple_of(kb * tk, tk)
        if masked:
            kpos, qpos = _positions(q_start, ks, tq, tk)
            strict = kpos < qpos
        at = _stage_lazily(stage, stage_static, chunks, off, tk, ks, masked) if lazy else ks
        heads = range(2 * n_pairs)
        zs = [_dot(kb_scr[pl.ds(at, tk), (h // 2) * LANES:(h // 2 + 1) * LANES], qs[h])
              for h in heads]
        log_bs, log_1ms, c_olds = [], [], []
        reach = None
        for h in heads:
            z = zs[h]
            log_b = jnp.minimum(z, 0.0) - jnp.log2(1.0 + jnp.exp2(-jnp.abs(z)))
            log_1m = log_b - z
            if masked:
                log_1m = jnp.where(strict, log_1m, 0.0)
            c_old = c_scr[h]
            c_new = c_old + jnp.sum(log_1m, axis=0, keepdims=True)
            c_scr[h] = c_new
            reach = c_new if reach is None else jnp.maximum(reach, c_new)
            log_bs.append(log_b)
            log_1ms.append(log_1m.astype(BF16))
            c_olds.append(c_old)
        rests = [_dot(tri_ref[...], log_1ms[h]) for h in heads]
        weights = []
        for h in heads:
            arg = log_bs[h] + (rests[h] + c_olds[h])
            if masked:
                arg = jnp.where(strict, arg, NEG)
            weights.append(jnp.exp2(arg).astype(BF16))
        pvs = [_dot(vt_scr[h * HEAD_DIM:(h + 1) * HEAD_DIM, pl.ds(at, tk)], weights[h]) for h in heads]
        for h in heads:
            acc_scr[h] = acc_scr[h] + pvs[h]
        return jnp.max(reach) >= EXP2_ZERO_BELOW

    _walk_key_blocks(step, q_start, tk)
    o_ref[0] = jnp.concatenate([acc_scr[h] for h in range(2 * n_pairs)], axis=0).T


def _attn_geometry(q_t, new, past, layer):
    b, width, tq_total = q_t.shape
    n_new = new[0].shape[3]
    n_past = 0 if past is None else past[0].shape[3]
    tk = KEY_BLOCK
    tq = 256 if tq_total % 256 == 0 else LANES
    assert tq_total % tq == 0 and tk % tq == 0 and n_past % tq == 0 and n_new >= tq_total
    chunks = _key_chunks(n_past, n_new, tk)
    tk_total = -(-(n_past + n_new) // tk) * tk
    gw = PAIRS_PER_STEP * LANES
    grid = (b, width // gw, tq_total // tq)
    q_spec = pl.BlockSpec((1, gw, tq), lambda i, g, j: (i, g, j))
    o_spec = pl.BlockSpec((1, tq, gw), lambda i, g, j: (i, j, g))
    kv_ins, kv_specs = [], []
    for src in ([past] if past is not None else []) + [new]:
        n = src[0].shape[3]
        kv_ins += list(src)
        kv_specs += [pl.BlockSpec((1, 1, gw, n), lambda i, g, j: (layer, i, g, 0),
                                  pipeline_mode=pl.Buffered(1))] * 2
    lazy = tq_total == tq and n_past > 0 and n_new <= tk
    return dict(b=b, width=width, tq_total=tq_total, tq=tq, tk=tk, off=n_past, chunks=chunks, lazy=lazy,
                tk_total=tk_total, gw=gw, grid=grid, q_spec=q_spec, o_spec=o_spec,
                kv_ins=kv_ins, kv_specs=kv_specs)


def _fox_attention(q_t, new, past, layer, f2, qk_bound):
    G = _attn_geometry(q_t, new, past, layer)
    b, tq, tk, tk_total, gw = G["b"], G["tq"], G["tk"], G["tk_total"], G["gw"]
    n_heads = G["width"] // HEAD_DIM
    assert f2.shape == (b, n_heads, tk_total)
    f_end = f2[:, :, tk - 1::tk].reshape(-1)
    heads = gw // HEAD_DIM
    return pl.pallas_call(
        functools.partial(_fox_kernel, tq=tq, tk=tk, off=G["off"], n_heads=n_heads, chunks=G["chunks"],
                          lazy=G["lazy"]),
        grid=G["grid"],
        in_specs=[_SMEM, _SMEM, G["q_spec"]] + G["kv_specs"]
                 + [pl.BlockSpec((1, n_heads, tk_total), lambda i, g, j: (i, 0, 0))],
        out_specs=G["o_spec"],
        out_shape=jax.ShapeDtypeStruct((b, G["tq_total"], G["width"]), F32),
        scratch_shapes=[pltpu.VMEM((heads, tk if G["lazy"] else tk_total, LANES), BF16),
                        pltpu.VMEM((gw, tk if G["lazy"] else tk_total), BF16),
                        pltpu.VMEM((heads, 1, tq), F32), pltpu.VMEM((heads, 1, tq), F32),
                        pltpu.VMEM((heads, HEAD_DIM, tq), F32)],
        compiler_params=_params(3),
        name="fox_attention",
    )(qk_bound, f_end, q_t, *G["kv_ins"], f2)


def _sb_attention(q_t, new, past, layer):
    G = _attn_geometry(q_t, new, past, layer)
    b, tq, tk, tk_total, gw = G["b"], G["tq"], G["tk"], G["tk_total"], G["gw"]
    r = lax.broadcasted_iota(jnp.int32, (tk, tk), 0)
    c = lax.broadcasted_iota(jnp.int32, (tk, tk), 1)
    tri = (c > r).astype(BF16)
    heads = gw // HEAD_DIM
    return pl.pallas_call(
        functools.partial(_sb_kernel, tq=tq, tk=tk, off=G["off"], chunks=G["chunks"], lazy=G["lazy"]),
        grid=G["grid"],
        in_specs=[G["q_spec"]] + G["kv_specs"] + [_resident(tri)],
        out_specs=G["o_spec"],
        out_shape=jax.ShapeDtypeStruct((b, G["tq_total"], G["width"]), F32),
        scratch_shapes=[pltpu.VMEM((tk if G["lazy"] else tk_total, gw), BF16),
                        pltpu.VMEM((gw, tk if G["lazy"] else tk_total), BF16),
                        pltpu.VMEM((heads, 1, tq), F32), pltpu.VMEM((heads, HEAD_DIM, tq), F32)],
        compiler_params=_params(3),
        name="sb_attention",
    )(q_t, *G["kv_ins"], tri)


def _mix_out_cross_kernel(of_ref, os_ref, x_ref, mk_ref, mv_ref, gf_ref, gs_ref, w_ref,
                          g_ref, wq_ref, gq_ref, wo_ref, y_ref):
    a = _rms(of_ref[0], gf_ref[...]).astype(BF16)
    b = _rms(os_ref[0], gs_ref[...]).astype(BF16)
    nf = a.shape[1]
    x = x_ref[0] + _dot(a, w_ref[0:nf, :]) + _dot(b, w_ref[nf:, :])
    h = _rms(x, g_ref[...]).astype(BF16)
    hd = wq_ref.shape[1] // MEM_HEADS
    scale = hd ** -0.5
    heads = [slice(i * hd, (i + 1) * hd) for i in range(MEM_HEADS)]
    q_all = _dot(h, wq_ref[...])
    qs = [(_rms(q_all[:, sl], gq_ref[...]) * scale).astype(BF16) for sl in heads]
    scores = [_dot_nt(q, mk_ref[0, :, sl].astype(BF16)) for q, sl in zip(qs, heads)]
    es = [jnp.exp(s - jnp.max(s, axis=1, keepdims=True)) for s in scores]
    pvs = [_dot(e.astype(BF16), mv_ref[0, :, sl].astype(BF16)) for e, sl in zip(es, heads)]
    o = jnp.concatenate([(pv / jnp.sum(e, axis=1, keepdims=True)).astype(BF16)
                         for pv, e in zip(pvs, es)], axis=1)
    y_ref[0] = x + _dot(o, wo_ref[...])


def _mix_out_cross(o_f, o_s, x, mem_k, mem_v, P):
    b, t, d = x.shape
    tm = min(512, t)
    row = lambda w: pl.BlockSpec((1, tm, w), lambda i, j: (i, j, 0))
    mem = pl.BlockSpec((1,) + mem_k.shape[1:], lambda i, j: (i, 0, 0))
    ins = [o_f, o_s, x, mem_k, mem_v, P["g_out_fox"], P["g_out_sb"], P["w_out"],
           P["g_cross"], P["w_mq"], P["g_mq"], P["w_mo"]]
    return pl.pallas_call(
        _mix_out_cross_kernel,
        grid=(b, t // tm),
        in_specs=[row(o_f.shape[2]), row(o_s.shape[2]), row(d), mem, mem] + [_resident(a) for a in ins[5:]],
        out_specs=row(d),
        out_shape=jax.ShapeDtypeStruct((b, t, d), F32),
        compiler_params=_params(2),
        name="mix_out_cross",
    )(*ins)


def _ffn_kernel(x_ref, g_ref, w1_ref, w2_ref, y_ref, *, chunk):
    x = x_ref[...]
    h = _rms(x, g_ref[...]).astype(BF16)
    acc = x
    for c in range(w1_ref.shape[1] // chunk):
        sl = slice(c * chunk, (c + 1) * chunk)
        u = jnp.square(jnp.maximum(_dot(h, w1_ref[:, sl]), 0.0)).astype(BF16)
        acc = acc + _dot(u, w2_ref[sl, :])
    y_ref[...] = acc


def _ffn(x2d, P):
    n, d = x2d.shape
    tm = min(512, n)
    row = pl.BlockSpec((tm, d), lambda i: (i, 0))
    once = lambda a: pl.BlockSpec(a.shape, lambda i: (0, 0), pipeline_mode=pl.Buffered(1))
    return pl.pallas_call(
        functools.partial(_ffn_kernel, chunk=1024),
        grid=(n // tm,),
        in_specs=[row, _resident(P["g_ffn"]), once(P["w_ff1"]), once(P["w_ff2"])],
        out_specs=row,
        out_shape=jax.ShapeDtypeStruct((n, d), F32),
        compiler_params=_params(1),
        name="ffn",
    )(x2d, P["g_ffn"], P["w_ff1"], P["w_ff2"])


def _mem_kv_kernel(m_ref, g_ref, wk_ref, wv_ref, gk_ref, k_ref, v_ref):
    h = _rms(m_ref[...], g_ref[...]).astype(BF16)
    hd = wk_ref.shape[1] // MEM_HEADS
    for i in range(MEM_HEADS):
        sl = slice(i * hd, (i + 1) * hd)
        k_ref[:, sl] = _rms(_dot(h, wk_ref[:, sl]), gk_ref[...])
    v_ref[...] = _dot(h, wv_ref[...])


def _memory_kv(mem2d, P):
    n, d = mem2d.shape
    w = P["w_mk"].shape[1]
    tm = min(256, n)
    row = lambda c: pl.BlockSpec((tm, c), lambda i: (i, 0))
    ins = [mem2d, P["g_mem"], P["w_mk"], P["w_mv"], P["g_mk"]]
    return pl.pallas_call(
        _mem_kv_kernel,
        grid=(n // tm,),
        in_specs=[row(d)] + [_resident(a) for a in ins[1:]],
        out_specs=[row(w), row(w)],
        out_shape=[jax.ShapeDtypeStruct((n, w), F32)] * 2,
        compiler_params=_params(1),
        name="memory_kv",
    )(*ins)


def _layer(x, mem_k, mem_v, past, P, layer, depth, stacks):
    b, t, d = x.shape
    qf_t, qs_t, stacks = _proj_in(x, P, layer, depth, stacks)
    kf, vf, ks, vs, lf = stacks
    lf_t = lf[layer]
    if past is not None:
        lf_t = jnp.concatenate([past[2][layer], lf_t], axis=2)
    pad = -lf_t.shape[2] % KEY_BLOCK
    f2 = _cumsum_logf(jnp.pad(lf_t, ((0, 0), (0, 0), (0, pad))))
    o_f = _fox_attention(qf_t, (kf, vf), None if past is None else (past[0], past[1]), layer, f2, P["qk_bound"])
    o_s = _sb_attention(qs_t, (ks, vs), None if past is None else (past[3], past[4]), layer)
    y = _mix_out_cross(o_f, o_s, x, mem_k, mem_v, P)
    y = _ffn(y.reshape(b * t, d), P)
    return y.reshape(b, t, d), stacks


def _layer_params(l, g_mix, w_in, b_forget, g_fox_q, g_fox_k, g_out_fox, g_out_sb, w_out,
                  g_cross, g_mem, w_mq, w_mk, w_mv, g_mq, g_mk, w_mo, g_ffn, w_ff1, w_ff2):
    fox_w = g_out_fox.shape[1]
    sb_w = g_out_sb.shape[1]
    n_heads = b_forget.shape[1]
    assert fox_w == sb_w and n_heads == SUBLANES and g_fox_q.shape[1] == HEAD_DIM
    w = w_in[l]
    o_sb = 3 * fox_w + n_heads
    w_t = jnp.concatenate(
        [w[:, :3 * fox_w], w[:, o_sb:o_sb + 3 * sb_w],
         jnp.pad(w[:, 3 * fox_w:o_sb], ((0, 0), (0, 2 * SUBLANES - n_heads)))], axis=1).T.astype(BF16)
    row = lambda a: a[l].reshape(1, -1).astype(F32)
    col = lambda a: a[l].reshape(-1, 1).astype(F32)
    qk_bound = (LOG2E * HEAD_DIM ** 0.5 * 1.01) * jnp.max(jnp.abs(g_fox_q[l])) * jnp.max(jnp.abs(g_fox_k[l]))
    return dict(
        qk_bound=qk_bound.reshape(1).astype(F32),
        g_mix=row(g_mix), w_in_t=w_t,
        gq=jnp.tile(col(g_fox_q), (fox_w // HEAD_DIM, 1)),
        gk=jnp.tile(col(g_fox_k), (fox_w // HEAD_DIM, 1)),
        b_forget=col(b_forget),
        g_out_fox=row(g_out_fox), g_out_sb=row(g_out_sb), w_out=w_out[l].astype(BF16),
        g_cross=row(g_cross), w_mq=w_mq[l].astype(BF16), g_mq=row(g_mq), w_mo=w_mo[l].astype(BF16),
        g_mem=row(g_mem), w_mk=w_mk[l].astype(BF16), w_mv=w_mv[l].astype(BF16), g_mk=row(g_mk),
        g_ffn=row(g_ffn), w_ff1=w_ff1[l].astype(BF16), w_ff2=w_ff2[l].astype(BF16))


def _feature_major(cache):
    depth, b, t = cache.shape[:3]
    return jnp.moveaxis(cache.reshape(depth, b, t, -1), 2, 3)


def _time_major(stack, t, hd):
    depth, b, w, _ = stack.shape
    return jnp.moveaxis(stack[..., :t].reshape(depth, b, w // hd, hd, t), 4, 2)


def kernel(x_prompt, x_sample, mem_prompt, cache_fox_k, cache_fox_v, cache_fox_logf, cache_sb_k, cache_sb_v, cache_mem_k, cache_mem_v, g_mix, w_in, b_forget, g_fox_q, g_fox_k, g_out_fox, g_out_sb, w_out, g_cross, g_mem, w_mq, w_mk, w_mv, g_mq, g_mk, w_mo, g_ffn, w_ff1, w_ff2):
    depth = g_mix.shape[0]
    bp, tp, d = x_prompt.shape
    bs, ts, _ = x_sample.shape
    n_mem = mem_prompt.shape[1]
    n_heads = b_forget.shape[1]
    ts_pad = -(-ts // LANES) * LANES
    xp = x_prompt
    xs = jnp.pad(x_sample, ((0, 0), (0, ts_pad - ts), (0, 0)))
    mem2d = mem_prompt.reshape(bp * n_mem, d)
    past = (_feature_major(cache_fox_k), _feature_major(cache_fox_v),
            jnp.moveaxis(cache_fox_logf.astype(F32), 2, 3),
            _feature_major(cache_sb_k), _feature_major(cache_sb_v))
    stacks_p = stacks_s = None
    p_mk, p_mv = [], []
    for l in range(depth):
        P = _layer_params(l, g_mix, w_in, b_forget, g_fox_q, g_fox_k, g_out_fox, g_out_sb, w_out,
                          g_cross, g_mem, w_mq, w_mk, w_mv, g_mq, g_mk, w_mo, g_ffn, w_ff1, w_ff2)
        mk, mv = _memory_kv(mem2d, P)
        mk = mk.reshape(bp, n_mem, -1)
        mv = mv.reshape(bp, n_mem, -1)
        p_mk.append(mk)
        p_mv.append(mv)
        xp, stacks_p = _layer(xp, mk, mv, None, P, l, depth, stacks_p)
        xs, stacks_s = _layer(xs, cache_mem_k[l].reshape(bs, n_mem, -1), cache_mem_v[l].reshape(bs, n_mem, -1),
                              past, P, l, depth, stacks_s)

    mem_hd = p_mk[0].shape[-1] // MEM_HEADS
    p_fk, p_fv, p_sk, p_sv, p_lf = stacks_p
    s_fk, s_fv, s_sk, s_sv, s_lf = stacks_s
    return (xp, xs[:, :ts],
            _time_major(p_fk, tp, HEAD_DIM), _time_major(p_fv, tp, HEAD_DIM),
            jnp.moveaxis(p_lf, 2, 3),
            _time_major(p_sk, tp, HEAD_DIM), _time_major(p_sv, tp, HEAD_DIM),
            jnp.stack(p_mk).reshape(depth, bp, n_mem, MEM_HEADS, mem_hd),
            jnp.stack(p_mv).reshape(depth, bp, n_mem, MEM_HEADS, mem_hd),
            _time_major(s_fk, ts, HEAD_DIM), _time_major(s_fv, ts, HEAD_DIM),
            jnp.moveaxis(s_lf[..., :ts], 2, 3),
            _time_major(s_sk, ts, HEAD_DIM), _time_major(s_sv, ts, HEAD_DIM))
```

```python
import functools
import math

import jax
import jax.numpy as jnp
from jax import lax
from jax.experimental import pallas as pl
from jax.experimental.pallas import tpu as pltpu

EPS = 1e-6
NEG = -1e30
HEAD_DIM = 64
MEM_HEADS = 4
LANES = 128
SUBLANES = 8
VMEM_LIMIT_BYTES = 56 * 2**20
LOG2E = math.log2(math.e)
EXP2_ZERO_BELOW = -106.0 * LOG2E
KEY_BLOCK = 256
PAIRS_PER_STEP = 4
F32 = jnp.float32
BF16 = jnp.bfloat16


def _dot(a, b):
    return jnp.dot(a, b, preferred_element_type=F32)


def _dot_nt(a, b):
    return lax.dot_general(a, b, (((1,), (1,)), ((), ())), preferred_element_type=F32)


def _rms(x, g):
    return x * lax.rsqrt(jnp.mean(x * x, axis=-1, keepdims=True) + EPS) * g


def _split3(x):
    hi = x.astype(BF16)
    r = x - hi.astype(F32)
    mid = r.astype(BF16)
    lo = (r - mid.astype(F32)).astype(BF16)
    return hi, mid, lo


def _params(n_grid_dims):
    return pltpu.CompilerParams(
        dimension_semantics=("arbitrary",) * n_grid_dims,
        vmem_limit_bytes=VMEM_LIMIT_BYTES)


def _resident(arr):
    nd = arr.ndim
    return pl.BlockSpec(arr.shape, lambda *_: (0,) * nd)


_SMEM = pl.BlockSpec(memory_space=pltpu.SMEM)
_ANY = pl.BlockSpec(memory_space=pl.ANY)


def _proj_in_kernel(*refs, width, n_prev):
    x_ref, g_ref, wt_ref, gq_ref, gk_ref, bf_ref = refs[:6]
    qf_ref, qs_ref, kf_ref, vf_ref, ks_ref, vs_ref, lf_ref = refs[6 + n_prev:]
    ht = _rms(x_ref[0], g_ref[...]).T.astype(BF16)
    q_scale = LOG2E * HEAD_DIM ** -0.5

    def seg(i, rows=width):
        return _dot(wt_ref[i * width:i * width + rows, :], ht)

    def head_norm(y, g_col, store):
        for h in range(width // HEAD_DIM):
            rows = slice(h * HEAD_DIM, (h + 1) * HEAD_DIM)
            yh = y[rows, :]
            ms = jnp.mean(yh * yh, axis=0, keepdims=True)
            store(rows, yh * lax.rsqrt(ms + EPS) * g_col[rows, :])

    def store_qf(rows, y):
        qf_ref[0, rows, :] = (y * q_scale).astype(BF16)

    def store_kf(rows, y):
        kf_ref[0, 0, rows, :] = y

    head_norm(seg(0), gq_ref[...], store_qf)
    head_norm(seg(1), gk_ref[...], store_kf)
    vf_ref[0, 0] = seg(2)
    qs_ref[0] = (seg(3) * q_scale).astype(BF16)
    ks_ref[0, 0] = seg(4)
    vs_ref[0, 0] = seg(5)
    n_gates = lf_ref.shape[2]
    lf_ref[0, 0] = jax.nn.log_sigmoid(seg(6, 2 * SUBLANES)[0:n_gates, :] + bf_ref[...])


def _proj_in(x, P, layer, depth, prev):
    b, t, d = x.shape
    width = P["gq"].shape[0]
    n_gates = P["b_forget"].shape[0]
    tm = min(512, t)
    n_prev = 0 if prev is None else len(prev)
    ins = [x, P["g_mix"], P["w_in_t"], P["gq"], P["gk"], P["b_forget"]]
    q_spec = pl.BlockSpec((1, width, tm), lambda i, j: (i, 0, j))
    kv_spec = pl.BlockSpec((1, 1, width, tm), lambda i, j: (layer, i, 0, j))
    lf_spec = pl.BlockSpec((1, 1, n_gates, tm), lambda i, j: (layer, i, 0, j))
    sds = jax.ShapeDtypeStruct
    kv_shape = sds((depth, b, width, t), F32)
    outs = pl.pallas_call(
        functools.partial(_proj_in_kernel, width=width, n_prev=n_prev),
        grid=(b, t // tm),
        in_specs=[pl.BlockSpec((1, tm, d), lambda i, j: (i, j, 0))]
                 + [_resident(a) for a in ins[1:]] + [_ANY] * n_prev,
        out_specs=[q_spec, q_spec] + [kv_spec] * 4 + [lf_spec],
        out_shape=[sds((b, width, t), BF16)] * 2 + [kv_shape] * 4 + [sds((depth, b, n_gates, t), F32)],
        input_output_aliases={len(ins) + i: 2 + i for i in range(n_prev)},
        compiler_params=_params(2),
        name="proj_in",
    )(*ins, *(prev or ()))
    return outs[0], outs[1], tuple(outs[2:])


def _cumsum_kernel(lf_ref, tri_ref, f_ref, carry_ref):
    @pl.when(pl.program_id(0) == 0)
    def _():
        carry_ref[...] = jnp.zeros_like(carry_ref)

    tri = tri_ref[...]
    f = sum(_dot(part, tri) for part in _split3(lf_ref[...])) + carry_ref[...]
    f_ref[...] = f * LOG2E
    tb = f.shape[1]
    carry_ref[...] = f[:, tb - 1:tb]


def _cumsum_logf(lf_t):
    b, n_gates, t = lf_t.shape
    rows = b * n_gates
    tb = 512 if t % 512 == 0 else KEY_BLOCK
    r = lax.broadcasted_iota(jnp.int32, (tb, tb), 0)
    c = lax.broadcasted_iota(jnp.int32, (tb, tb), 1)
    tri = (r <= c).astype(BF16)
    spec = pl.BlockSpec((rows, tb), lambda j: (0, j))
    return pl.pallas_call(
        _cumsum_kernel,
        grid=(t // tb,),
        in_specs=[spec, _resident(tri)],
        out_specs=spec,
        out_shape=jax.ShapeDtypeStruct((rows, t), F32),
        scratch_shapes=[pltpu.VMEM((rows, 1), F32)],
        compiler_params=_params(1),
        name="cumsum_logf",
    )(lf_t.reshape(rows, t), tri).reshape(b, n_gates, t)


def _key_chunks(n_past, n_new, tk):
    assert n_past % tk == 0
    new_src = 1 if n_past else 0
    chunks = [(0, s, tk, s) for s in range(0, n_past, tk)]
    chunks += [(new_src, s, min(tk, n_new - s), n_past + s) for s in range(0, n_new, tk)]
    return chunks


def _positions(q_start, ks, tq, tk):
    kpos = ks + lax.broadcasted_iota(jnp.int32, (tk, tq), 0)
    qpos = q_start + lax.broadcasted_iota(jnp.int32, (tk, tq), 1)
    return kpos, qpos


def _walk_key_blocks(step, q_start, tk):
    kb_diag = q_start // tk
    go = step(kb_diag, True)

    def cond(carry):
        kb, go = carry
        return jnp.logical_and(kb >= 0, go)

    def body(carry):
        kb, _ = carry
        return kb - 1, step(kb, False)

    lax.while_loop(cond, body, (kb_diag - 1, go))


def _stage_lazily(stage, stage_static, chunks, off, tk, ks, masked):
    if masked:
        stage_static([c for c in chunks if c[3] >= off], off, tk)
    else:
        stage(0, ks, tk, 0, ks)
    return 0


def _pick_row(block, idx):
    row = lax.broadcasted_iota(jnp.int32, block.shape, 0)
    return jnp.sum(jnp.where(row == idx, block, 0.0), axis=0, keepdims=True)


def _gate_rows(f_row, key_side):
    n = f_row.shape[1]
    parts = [p.astype(F32) for p in _split3(-f_row if key_side else f_row)]
    ones = [jnp.ones((1, n), F32)] * 3
    row = lax.broadcasted_iota(jnp.int32, (SUBLANES, n), 0)
    out = jnp.zeros((SUBLANES, n), F32)
    for i, v in enumerate(parts + ones if key_side else ones + parts):
        out = jnp.where(row == i, v, out)
    return out


def _with_gate_rows(x_t, gates, head):
    n = x_t.shape[1]
    fill = jnp.zeros((HEAD_DIM - SUBLANES, n), F32)
    if head == 0:
        return jnp.concatenate([x_t[:HEAD_DIM], gates, fill], axis=0)
    return jnp.concatenate([gates, fill, x_t[HEAD_DIM:]], axis=0)


def _fox_kernel(*refs, tq, tk, off, n_heads, chunks, lazy):
    n_src = 1 + max(c[0] for c in chunks)
    bound_ref, fend_ref, q_ref = refs[:3]
    kv_refs = refs[3:3 + 2 * n_src]
    f_ref, o_ref, k_scr, vt_scr, m_scr, l_scr, acc_scr = refs[3 + 2 * n_src:]
    b = pl.program_id(0)
    group = pl.program_id(1)
    qi = pl.program_id(2)
    n_kb = f_ref.shape[2] // tk
    n_pairs = vt_scr.shape[0] // LANES
    first_head = 2 * n_pairs * group

    def stage(src, start, w, at, pos):
        k_ref, v_ref = kv_refs[2 * src], kv_refs[2 * src + 1]
        vt_scr[:, at:at + w] = v_ref[0, 0, :, pl.ds(start, w)].astype(BF16)
        f_blk = f_ref[0, :, pl.ds(pos, w)]
        for p in range(n_pairs):
            k_t = k_ref[0, 0, p * LANES:(p + 1) * LANES, pl.ds(start, w)]
            for j in range(2):
                gates = _gate_rows(_pick_row(f_blk, first_head + 2 * p + j), True)
                k_scr[2 * p + j, at:at + w, :] = _with_gate_rows(k_t, gates, j).T.astype(BF16)

    def stage_static(pieces, base, size):
        filled = max(dst + w for _, _, w, dst in pieces) - base
        if filled < size:
            k_scr[:, filled:size, :] = jnp.zeros((2 * n_pairs, size - filled, LANES), BF16)
            vt_scr[:, filled:size] = jnp.zeros((n_pairs * LANES, size - filled), BF16)
        for src, start, w, dst in pieces:
            stage(src, start, w, dst - base, dst)

    if not lazy:
        @pl.when(qi == 0)
        def _():
            stage_static(chunks, 0, n_kb * tk)

    q_start = off + qi * tq
    qs, fqs = [], []
    for p in range(n_pairs):
        q_t = q_ref[0, p * LANES:(p + 1) * LANES, :].astype(F32)
        for j in range(2):
            fq = _pick_row(f_ref[0, :, pl.ds(pl.multiple_of(q_start, LANES), tq)], first_head + 2 * p + j)
            fqs.append(fq)
            qs.append(_with_gate_rows(q_t, _gate_rows(fq, False), j).astype(BF16))

    m_scr[...] = jnp.full_like(m_scr, NEG)
    l_scr[...] = jnp.zeros_like(l_scr)
    acc_scr[...] = jnp.zeros_like(acc_scr)

    def step(kb, masked):
        ks = pl.multiple_of(kb * tk, tk)
        if masked:
            kpos, qpos = _positions(q_start, ks, tq, tk)
            visible = kpos <= qpos
        at = _stage_lazily(stage, stage_static, chunks, off, tk, ks, masked) if lazy else ks
        heads = range(2 * n_pairs)
        scores = [_dot(k_scr[h, pl.ds(at, tk), :], qs[h]) for h in heads]
        go = False
        alphas, probs = [], []
        for h in heads:
            s = scores[h]
            if masked:
                s = jnp.where(visible, s, NEG)
            m_prev = m_scr[h]
            m_new = jnp.maximum(m_prev, jnp.max(s, axis=0, keepdims=True))
            alpha = jnp.exp2(m_prev - m_new)
            p = jnp.exp2(s - m_new)
            l_scr[h] = alpha * l_scr[h] + jnp.sum(p, axis=0, keepdims=True)
            m_scr[h] = m_new
            alphas.append(alpha)
            probs.append(p.astype(BF16))
            f_end = fend_ref[(b * n_heads + first_head + h) * n_kb + jnp.maximum(kb - 1, 0)]
            reach = jnp.max(fqs[h] - m_new) + bound_ref[0] - f_end
            go = jnp.logical_or(go, reach >= EXP2_ZERO_BELOW)
        pvs = [_dot(vt_scr[h * HEAD_DIM:(h + 1) * HEAD_DIM, pl.ds(at, tk)], probs[h]) for h in heads]
        for h in heads:
            acc_scr[h] = acc_scr[h] * alphas[h] + pvs[h]
        return go

    _walk_key_blocks(step, q_start, tk)
    o_t = jnp.concatenate([acc_scr[h] / l_scr[h] for h in range(2 * n_pairs)], axis=0)
    o_ref[0] = o_t.T


def _sb_kernel(*refs, tq, tk, off, chunks, lazy):
    n_src = 1 + max(c[0] for c in chunks)
    q_ref = refs[0]
    kv_refs = refs[1:1 + 2 * n_src]
    tri_ref, o_ref, kb_scr, vt_scr, c_scr, acc_scr = refs[1 + 2 * n_src:]
    qi = pl.program_id(2)
    n_pairs = vt_scr.shape[0] // LANES
    row_lo = lax.broadcasted_iota(jnp.int32, (LANES, 1), 0) < HEAD_DIM

    def stage(src, start, w, at, pos):
        del pos
        k_ref, v_ref = kv_refs[2 * src], kv_refs[2 * src + 1]
        vt_scr[:, at:at + w] = v_ref[0, 0, :, pl.ds(start, w)].astype(BF16)
        for p in range(n_pairs):
            lanes = slice(p * LANES, (p + 1) * LANES)
            kb_scr[at:at + w, lanes] = k_ref[0, 0, lanes, pl.ds(start, w)].T.astype(BF16)

    def stage_static(pieces, base, size):
        filled = max(dst + w for _, _, w, dst in pieces) - base
        if filled < size:
            kb_scr[filled:size, :] = jnp.zeros((size - filled, n_pairs * LANES), BF16)
            vt_scr[:, filled:size] = jnp.zeros((n_pairs * LANES, size - filled), BF16)
        for src, start, w, dst in pieces:
            stage(src, start, w, dst - base, dst)

    if not lazy:
        @pl.when(qi == 0)
        def _():
            stage_static(chunks, 0, vt_scr.shape[1])

    qs = []
    for p in range(n_pairs):
        q_t = q_ref[0, p * LANES:(p + 1) * LANES, :]
        zero = jnp.zeros_like(q_t)
        qs += [jnp.where(row_lo, q_t, zero), jnp.where(row_lo, zero, q_t)]
    c_scr[...] = jnp.zeros_like(c_scr)
    acc_scr[...] = jnp.zeros_like(acc_scr)
    q_start = off + qi * tq

    def step(kb, masked):
        ks = pl.multiple_of(kb * tk, tk)
        if masked:
            kpos, qpos = _positions(q_start, ks, tq, tk)
            strict = kpos < qpos
        at = _stage_lazily(stage, stage_static, chunks, off, tk, ks, masked) if lazy else ks
        heads = range(2 * n_pairs)
        zs = [_dot(kb_scr[pl.ds(at, tk), (h // 2) * LANES:(h // 2 + 1) * LANES], qs[h])
              for h in heads]
        log_bs, log_1ms, c_olds = [], [], []
        reach = None
        for h in heads:
            z = zs[h]
            log_b = jnp.minimum(z, 0.0) - jnp.log2(1.0 + jnp.exp2(-jnp.abs(z)))
            log_1m = log_b - z
            if masked:
                log_1m = jnp.where(strict, log_1m, 0.0)
            c_old = c_scr[h]
            c_new = c_old + jnp.sum(log_1m, axis=0, keepdims=True)
            c_scr[h] = c_new
            reach = c_new if reach is None else jnp.maximum(reach, c_new)
            log_bs.append(log_b)
            log_1ms.append(log_1m.astype(BF16))
            c_olds.append(c_old)
        rests = [_dot(tri_ref[...], log_1ms[h]) for h in heads]
        weights = []
        for h in heads:
            arg = log_bs[h] + (rests[h] + c_olds[h])
            if masked:
                arg = jnp.where(strict, arg, NEG)
            weights.append(jnp.exp2(arg).astype(BF16))
        pvs = [_dot(vt_scr[h * HEAD_DIM:(h + 1) * HEAD_DIM, pl.ds(at, tk)], weights[h]) for h in heads]
        for h in heads:
            acc_scr[h] = acc_scr[h] + pvs[h]
        return jnp.max(reach) >= EXP2_ZERO_BELOW

    _walk_key_blocks(step, q_start, tk)
    o_ref[0] = jnp.concatenate([acc_scr[h] for h in range(2 * n_pairs)], axis=0).T


def _attn_geometry(q_t, new, past, layer):
    b, width, tq_total = q_t.shape
    n_new = new[0].shape[3]
    n_past = 0 if past is None else past[0].shape[3]
    tk = KEY_BLOCK
    tq = 256 if tq_total % 256 == 0 else LANES
    assert tq_total % tq == 0 and tk % tq == 0 and n_past % tq == 0 and n_new >= tq_total
    chunks = _key_chunks(n_past, n_new, tk)
    tk_total = -(-(n_past + n_new) // tk) * tk
    gw = PAIRS_PER_STEP * LANES
    grid = (b, width // gw, tq_total // tq)
    lazy = tq_total == tq and n_past > 0 and n_new <= tk
    q_spec = pl.BlockSpec((1, gw, tq), lambda i, g, j: (i, g, j))
    o_spec = pl.BlockSpec((1, tq, gw), lambda i, g, j: (i, j, g))
    kv_ins, kv_specs = [], []
    for src in ([past] if past is not None else []) + [new]:
        n = src[0].shape[3]
        kv_ins += list(src)
        mode = {} if lazy else dict(pipeline_mode=pl.Buffered(1))
        kv_specs += [pl.BlockSpec((1, 1, gw, n), lambda i, g, j: (layer, i, g, 0), **mode)] * 2
    return dict(b=b, width=width, tq_total=tq_total, tq=tq, tk=tk, off=n_past, chunks=chunks, lazy=lazy,
                tk_total=tk_total, gw=gw, grid=grid, q_spec=q_spec, o_spec=o_spec,
                kv_ins=kv_ins, kv_specs=kv_specs)


def _fox_attention(q_t, new, past, layer, f2, qk_bound):
    G = _attn_geometry(q_t, new, past, layer)
    b, tq, tk, tk_total, gw = G["b"], G["tq"], G["tk"], G["tk_total"], G["gw"]
    n_heads = G["width"] // HEAD_DIM
    assert f2.shape == (b, n_heads, tk_total)
    f_end = f2[:, :, tk - 1::tk].reshape(-1)
    heads = gw // HEAD_DIM
    return pl.pallas_call(
        functools.partial(_fox_kernel, tq=tq, tk=tk, off=G["off"], n_heads=n_heads, chunks=G["chunks"],
                          lazy=G["lazy"]),
        grid=G["grid"],
        in_specs=[_SMEM, _SMEM, G["q_spec"]] + G["kv_specs"]
                 + [pl.BlockSpec((1, n_heads, tk_total), lambda i, g, j: (i, 0, 0))],
        out_specs=G["o_spec"],
        out_shape=jax.ShapeDtypeStruct((b, G["tq_total"], G["width"]), F32),
        scratch_shapes=[pltpu.VMEM((heads, tk if G["lazy"] else tk_total, LANES), BF16),
                        pltpu.VMEM((gw, tk if G["lazy"] else tk_total), BF16),
                        pltpu.VMEM((heads, 1, tq), F32), pltpu.VMEM((heads, 1, tq), F32),
                        pltpu.VMEM((heads, HEAD_DIM, tq), F32)],
        compiler_params=_params(3),
        name="fox_attention",
    )(qk_bound, f_end, q_t, *G["kv_ins"], f2)


def _sb_attention(q_t, new, past, layer):
    G = _attn_geometry(q_t, new, past, layer)
    b, tq, tk, tk_total, gw = G["b"], G["tq"], G["tk"], G["tk_total"], G["gw"]
    r = lax.broadcasted_iota(jnp.int32, (tk, tk), 0)
    c = lax.broadcasted_iota(jnp.int32, (tk, tk), 1)
    tri = (c > r).astype(BF16)
    heads = gw // HEAD_DIM
    return pl.pallas_call(
        functools.partial(_sb_kernel, tq=tq, tk=tk, off=G["off"], chunks=G["chunks"], lazy=G["lazy"]),
        grid=G["grid"],
        in_specs=[G["q_spec"]] + G["kv_specs"] + [_resident(tri)],
        out_specs=G["o_spec"],
        out_shape=jax.ShapeDtypeStruct((b, G["tq_total"], G["width"]), F32),
        scratch_shapes=[pltpu.VMEM((tk if G["lazy"] else tk_total, gw), BF16),
                        pltpu.VMEM((gw, tk if G["lazy"] else tk_total), BF16),
                        pltpu.VMEM((heads, 1, tq), F32), pltpu.VMEM((heads, HEAD_DIM, tq), F32)],
        compiler_params=_params(3),
        name="sb_attention",
    )(q_t, *G["kv_ins"], tri)


def _mix_out_cross_kernel(of_ref, os_ref, x_ref, mk_ref, mv_ref, gf_ref, gs_ref, w_ref,
                          g_ref, wq_ref, gq_ref, wo_ref, y_ref):
    a = _rms(of_ref[0], gf_ref[...]).astype(BF16)
    b = _rms(os_ref[0], gs_ref[...]).astype(BF16)
    nf = a.shape[1]
    x = x_ref[0] + _dot(a, w_ref[0:nf, :]) + _dot(b, w_ref[nf:, :])
    h = _rms(x, g_ref[...]).astype(BF16)
    hd = wq_ref.shape[1] // MEM_HEADS
    scale = hd ** -0.5
    heads = [slice(i * hd, (i + 1) * hd) for i in range(MEM_HEADS)]
    q_all = _dot(h, wq_ref[...])
    qs = [(_rms(q_all[:, sl], gq_ref[...]) * scale).astype(BF16) for sl in heads]
    scores = [_dot_nt(q, mk_ref[0, 0, :, sl].astype(BF16)) for q, sl in zip(qs, heads)]
    es = [jnp.exp(s - jnp.max(s, axis=1, keepdims=True)) for s in scores]
    pvs = [_dot(e.astype(BF16), mv_ref[0, 0, :, sl].astype(BF16)) for e, sl in zip(es, heads)]
    o = jnp.concatenate([(pv / jnp.sum(e, axis=1, keepdims=True)).astype(BF16)
                         for pv, e in zip(pvs, es)], axis=1)
    y_ref[0] = x + _dot(o, wo_ref[...])


def _mix_out_cross(o_f, o_s, x, mem_k, mem_v, layer, P):
    b, t, d = x.shape
    tm = min(512, t)
    row = lambda w: pl.BlockSpec((1, tm, w), lambda i, j: (i, j, 0))
    mem = pl.BlockSpec((1, 1) + mem_k.shape[2:], lambda i, j: (layer, i, 0, 0))
    ins = [o_f, o_s, x, mem_k, mem_v, P["g_out_fox"], P["g_out_sb"], P["w_out"],
           P["g_cross"], P["w_mq"], P["g_mq"], P["w_mo"]]
    return pl.pallas_call(
        _mix_out_cross_kernel,
        grid=(b, t // tm),
        in_specs=[row(o_f.shape[2]), row(o_s.shape[2]), row(d), mem, mem] + [_resident(a) for a in ins[5:]],
        out_specs=row(d),
        out_shape=jax.ShapeDtypeStruct((b, t, d), F32),
        compiler_params=_params(2),
        name="mix_out_cross",
    )(*ins)


def _ffn_kernel(x_ref, g_ref, w1_ref, w2_ref, y_ref, *, chunk):
    x = x_ref[...]
    h = _rms(x, g_ref[...]).astype(BF16)
    acc = x
    for c in range(w1_ref.shape[1] // chunk):
        sl = slice(c * chunk, (c + 1) * chunk)
        u = jnp.square(jnp.maximum(_dot(h, w1_ref[:, sl]), 0.0)).astype(BF16)
        acc = acc + _dot(u, w2_ref[sl, :])
    y_ref[...] = acc


def _ffn(x2d, P):
    n, d = x2d.shape
    tm = min(512, n)
    row = pl.BlockSpec((tm, d), lambda i: (i, 0))
    once = lambda a: pl.BlockSpec(a.shape, lambda i: (0, 0), pipeline_mode=pl.Buffered(1))
    return pl.pallas_call(
        functools.partial(_ffn_kernel, chunk=1024),
        grid=(n // tm,),
        in_specs=[row, _resident(P["g_ffn"]), once(P["w_ff1"]), once(P["w_ff2"])],
        out_specs=row,
        out_shape=jax.ShapeDtypeStruct((n, d), F32),
        compiler_params=_params(1),
        name="ffn",
    )(x2d, P["g_ffn"], P["w_ff1"], P["w_ff2"])


def _mem_kv_kernel(m_ref, g_ref, wk_ref, wv_ref, gk_ref, k_ref, v_ref):
    h = _rms(m_ref[...], g_ref[0]).astype(BF16)
    hd = wk_ref.shape[2] // MEM_HEADS
    for i in range(MEM_HEADS):
        sl = slice(i * hd, (i + 1) * hd)
        k_ref[0, :, sl] = _rms(_dot(h, wk_ref[0, :, sl]), gk_ref[0])
    v_ref[0] = _dot(h, wv_ref[0])


def _memory_kv(mem2d, g_mem, w_mk, w_mv, g_mk):
    n, d = mem2d.shape
    depth, _, w = w_mk.shape
    tm = min(256, n)
    per_layer = lambda a: pl.BlockSpec((1,) + a.shape[1:], lambda l, i: (l,) + (0,) * (a.ndim - 1))
    ins = [mem2d, g_mem.reshape(depth, 1, d).astype(F32), w_mk.astype(BF16), w_mv.astype(BF16),
           g_mk.reshape(depth, 1, -1).astype(F32)]
    out_spec = pl.BlockSpec((1, tm, w), lambda l, i: (l, i, 0))
    return pl.pallas_call(
        _mem_kv_kernel,
        grid=(depth, n // tm),
        in_specs=[pl.BlockSpec((tm, d), lambda l, i: (i, 0))] + [per_layer(a) for a in ins[1:]],
        out_specs=[out_spec, out_spec],
        out_shape=[jax.ShapeDtypeStruct((depth, n, w), F32)] * 2,
        compiler_params=_params(2),
        name="memory_kv",
    )(*ins)


def _layer(x, mem_k, mem_v, past, P, layer, depth, stacks):
    b, t, d = x.shape
    qf_t, qs_t, stacks = _proj_in(x, P, layer, depth, stacks)
    kf, vf, ks, vs, lf = stacks
    lf_t = lf[layer]
    if past is not None:
        lf_t = jnp.concatenate([past[2][layer], lf_t], axis=2)
    pad = -lf_t.shape[2] % KEY_BLOCK
    f2 = _cumsum_logf(jnp.pad(lf_t, ((0, 0), (0, 0), (0, pad))))
    o_f = _fox_attention(qf_t, (kf, vf), None if past is None else (past[0], past[1]), layer, f2, P["qk_bound"])
    o_s = _sb_attention(qs_t, (ks, vs), None if past is None else (past[3], past[4]), layer)
    y = _mix_out_cross(o_f, o_s, x, mem_k, mem_v, layer, P)
    y = _ffn(y.reshape(b * t, d), P)
    return y.reshape(b, t, d), stacks


def _layer_params(l, g_mix, w_in, b_forget, g_fox_q, g_fox_k, g_out_fox, g_out_sb, w_out,
                  g_cross, g_mem, w_mq, w_mk, w_mv, g_mq, g_mk, w_mo, g_ffn, w_ff1, w_ff2):
    fox_w = g_out_fox.shape[1]
    sb_w = g_out_sb.shape[1]
    n_heads = b_forget.shape[1]
    assert fox_w == sb_w and n_heads == SUBLANES and g_fox_q.shape[1] == HEAD_DIM
    w = w_in[l]
    o_sb = 3 * fox_w + n_heads
    w_t = jnp.concatenate(
        [w[:, :3 * fox_w], w[:, o_sb:o_sb + 3 * sb_w],
         jnp.pad(w[:, 3 * fox_w:o_sb], ((0, 0), (0, 2 * SUBLANES - n_heads)))], axis=1).T.astype(BF16)
    row = lambda a: a[l].reshape(1, -1).astype(F32)
    col = lambda a: a[l].reshape(-1, 1).astype(F32)
    qk_bound = (LOG2E * HEAD_DIM ** 0.5 * 1.01) * jnp.max(jnp.abs(g_fox_q[l])) * jnp.max(jnp.abs(g_fox_k[l]))
    return dict(
        qk_bound=qk_bound.reshape(1).astype(F32),
        g_mix=row(g_mix), w_in_t=w_t,
        gq=jnp.tile(col(g_fox_q), (fox_w // HEAD_DIM, 1)),
        gk=jnp.tile(col(g_fox_k), (fox_w // HEAD_DIM, 1)),
        b_forget=col(b_forget),
        g_out_fox=row(g_out_fox), g_out_sb=row(g_out_sb), w_out=w_out[l].astype(BF16),
        g_cross=row(g_cross), w_mq=w_mq[l].astype(BF16), g_mq=row(g_mq), w_mo=w_mo[l].astype(BF16),
        g_ffn=row(g_ffn), w_ff1=w_ff1[l].astype(BF16), w_ff2=w_ff2[l].astype(BF16))


def _feature_major(cache):
    depth, b, t = cache.shape[:3]
    return jnp.moveaxis(cache.reshape(depth, b, t, -1), 2, 3)


def _time_major(stack, t, hd):
    depth, b, w, _ = stack.shape
    return jnp.moveaxis(stack[..., :t].reshape(depth, b, w // hd, hd, t), 4, 2)


def kernel(x_prompt, x_sample, mem_prompt, cache_fox_k, cache_fox_v, cache_fox_logf, cache_sb_k, cache_sb_v, cache_mem_k, cache_mem_v, g_mix, w_in, b_forget, g_fox_q, g_fox_k, g_out_fox, g_out_sb, w_out, g_cross, g_mem, w_mq, w_mk, w_mv, g_mq, g_mk, w_mo, g_ffn, w_ff1, w_ff2):
    depth = g_mix.shape[0]
    bp, tp, d = x_prompt.shape
    bs, ts, _ = x_sample.shape
    n_mem = mem_prompt.shape[1]
    n_heads = b_forget.shape[1]
    ts_pad = -(-ts // LANES) * LANES
    xp = x_prompt
    xs = jnp.pad(x_sample, ((0, 0), (0, ts_pad - ts), (0, 0)))
    mem2d = mem_prompt.reshape(bp * n_mem, d)
    past = (_feature_major(cache_fox_k), _feature_major(cache_fox_v),
            jnp.moveaxis(cache_fox_logf.astype(F32), 2, 3),
            _feature_major(cache_sb_k), _feature_major(cache_sb_v))
    stacks_p = stacks_s = None
    p_mk, p_mv = [a.reshape(depth, bp, n_mem, -1) for a in _memory_kv(mem2d, g_mem, w_mk, w_mv, g_mk)]
    s_mk, s_mv = [a.reshape(depth, bs, n_mem, -1) for a in (cache_mem_k, cache_mem_v)]
    for l in range(depth):
        P = _layer_params(l, g_mix, w_in, b_forget, g_fox_q, g_fox_k, g_out_fox, g_out_sb, w_out,
                          g_cross, g_mem, w_mq, w_mk, w_mv, g_mq, g_mk, w_mo, g_ffn, w_ff1, w_ff2)
        xp, stacks_p = _layer(xp, p_mk, p_mv, None, P, l, depth, stacks_p)
        xs, stacks_s = _layer(xs, s_mk, s_mv, past, P, l, depth, stacks_s)

    mem_hd = p_mk.shape[-1] // MEM_HEADS
    p_fk, p_fv, p_sk, p_sv, p_lf = stacks_p
    s_fk, s_fv, s_sk, s_sv, s_lf = stacks_s
    return (xp, xs[:, :ts],
            _time_major(p_fk, tp, HEAD_DIM), _time_major(p_fv, tp, HEAD_DIM),
            jnp.moveaxis(p_lf, 2, 3),
            _time_major(p_sk, tp, HEAD_DIM), _time_major(p_sv, tp, HEAD_DIM),
            p_mk.reshape(depth, bp, n_mem, MEM_HEADS, mem_hd),
            p_mv.reshape(depth, bp, n_mem, MEM_HEADS, mem_hd),
            _time_major(s_fk, ts, HEAD_DIM), _time_major(s_fv, ts, HEAD_DIM),
            jnp.moveaxis(s_lf[..., :ts], 2, 3),
            _time_major(s_sk, ts, HEAD_DIM), _time_major(s_sv, ts, HEAD_DIM))
```

```python
import functools
import math

import jax
import jax.numpy as jnp
from jax import lax
from jax.experimental import pallas as pl
from jax.experimental.pallas import tpu as pltpu

EPS = 1e-6
NEG = -1e30
HEAD_DIM = 64
MEM_HEADS = 4
LANES = 128
SUBLANES = 8
VMEM_LIMIT_BYTES = 56 * 2**20
LOG2E = math.log2(math.e)
EXP2_ZERO_BELOW = -106.0 * LOG2E
KEY_BLOCK = 256
OLDER_KEYS_SB = 128
PAIRS_PER_STEP = 4
F32 = jnp.float32
BF16 = jnp.bfloat16


def _dot(a, b):
    return jnp.dot(a, b, preferred_element_type=F32)


def _dot_nt(a, b):
    return lax.dot_general(a, b, (((1,), (1,)), ((), ())), preferred_element_type=F32)


def _rms(x, g):
    return x * lax.rsqrt(jnp.mean(x * x, axis=-1, keepdims=True) + EPS) * g


def _split3(x):
    hi = x.astype(BF16)
    r = x - hi.astype(F32)
    mid = r.astype(BF16)
    lo = (r - mid.astype(F32)).astype(BF16)
    return hi, mid, lo


def _params(n_grid_dims):
    return pltpu.CompilerParams(
        dimension_semantics=("arbitrary",) * n_grid_dims,
        vmem_limit_bytes=VMEM_LIMIT_BYTES)


def _resident(arr):
    nd = arr.ndim
    return pl.BlockSpec(arr.shape, lambda *_: (0,) * nd)


_SMEM = pl.BlockSpec(memory_space=pltpu.SMEM)
_ANY = pl.BlockSpec(memory_space=pl.ANY)


def _proj_in_kernel(*refs, width, n_prev):
    x_ref, g_ref, wt_ref, gq_ref, gk_ref, bf_ref = refs[:6]
    qf_ref, qs_ref, kf_ref, vf_ref, ks_ref, vs_ref, lf_ref = refs[6 + n_prev:]
    ht = _rms(x_ref[0], g_ref[...]).T.astype(BF16)
    q_scale = LOG2E * HEAD_DIM ** -0.5

    def seg(i, rows=width):
        return _dot(wt_ref[i * width:i * width + rows, :], ht)

    def head_norm(y, g_col, store):
        for h in range(width // HEAD_DIM):
            rows = slice(h * HEAD_DIM, (h + 1) * HEAD_DIM)
            yh = y[rows, :]
            ms = jnp.mean(yh * yh, axis=0, keepdims=True)
            store(rows, yh * lax.rsqrt(ms + EPS) * g_col[rows, :])

    def store_qf(rows, y):
        qf_ref[0, rows, :] = (y * q_scale).astype(BF16)

    def store_kf(rows, y):
        kf_ref[0, 0, rows, :] = y

    head_norm(seg(0), gq_ref[...], store_qf)
    head_norm(seg(1), gk_ref[...], store_kf)
    vf_ref[0, 0] = seg(2)
    qs_ref[0] = (seg(3) * q_scale).astype(BF16)
    ks_ref[0, 0] = seg(4)
    vs_ref[0, 0] = seg(5)
    n_gates = lf_ref.shape[2]
    lf_ref[0, 0] = jax.nn.log_sigmoid(seg(6, 2 * SUBLANES)[0:n_gates, :] + bf_ref[...])


def _proj_in(x, P, layer, depth, prev):
    b, t, d = x.shape
    width = P["gq"].shape[0]
    n_gates = P["b_forget"].shape[0]
    tm = min(512, t)
    n_prev = 0 if prev is None else len(prev)
    ins = [x, P["g_mix"], P["w_in_t"], P["gq"], P["gk"], P["b_forget"]]
    q_spec = pl.BlockSpec((1, width, tm), lambda i, j: (i, 0, j))
    kv_spec = pl.BlockSpec((1, 1, width, tm), lambda i, j: (layer, i, 0, j))
    lf_spec = pl.BlockSpec((1, 1, n_gates, tm), lambda i, j: (layer, i, 0, j))
    sds = jax.ShapeDtypeStruct
    kv_shape = sds((depth, b, width, t), F32)
    outs = pl.pallas_call(
        functools.partial(_proj_in_kernel, width=width, n_prev=n_prev),
        grid=(b, t // tm),
        in_specs=[pl.BlockSpec((1, tm, d), lambda i, j: (i, j, 0))]
                 + [_resident(a) for a in ins[1:]] + [_ANY] * n_prev,
        out_specs=[q_spec, q_spec] + [kv_spec] * 4 + [lf_spec],
        out_shape=[sds((b, width, t), BF16)] * 2 + [kv_shape] * 4 + [sds((depth, b, n_gates, t), F32)],
        input_output_aliases={len(ins) + i: 2 + i for i in range(n_prev)},
        compiler_params=_params(2),
        name="proj_in",
    )(*ins, *(prev or ()))
    return outs[0], outs[1], tuple(outs[2:])


def _cumsum_kernel(lf_ref, tri_ref, f_ref, carry_ref):
    @pl.when(pl.program_id(0) == 0)
    def _():
        carry_ref[...] = jnp.zeros_like(carry_ref)

    tri = tri_ref[...]
    f = sum(_dot(part, tri) for part in _split3(lf_ref[...])) + carry_ref[...]
    f_ref[...] = f * LOG2E
    tb = f.shape[1]
    carry_ref[...] = f[:, tb - 1:tb]


def _cumsum_logf(lf_t):
    b, n_gates, t = lf_t.shape
    rows = b * n_gates
    tb = 512 if t % 512 == 0 else KEY_BLOCK
    r = lax.broadcasted_iota(jnp.int32, (tb, tb), 0)
    c = lax.broadcasted_iota(jnp.int32, (tb, tb), 1)
    tri = (r <= c).astype(BF16)
    spec = pl.BlockSpec((rows, tb), lambda j: (0, j))
    return pl.pallas_call(
        _cumsum_kernel,
        grid=(t // tb,),
        in_specs=[spec, _resident(tri)],
        out_specs=spec,
        out_shape=jax.ShapeDtypeStruct((rows, t), F32),
        scratch_shapes=[pltpu.VMEM((rows, 1), F32)],
        compiler_params=_params(1),
        name="cumsum_logf",
    )(lf_t.reshape(rows, t), tri).reshape(b, n_gates, t)


def _key_chunks(n_past, n_new, tk):
    assert n_past % tk == 0
    new_src = 1 if n_past else 0
    chunks = [(0, s, tk, s) for s in range(0, n_past, tk)]
    chunks += [(new_src, s, min(tk, n_new - s), n_past + s) for s in range(0, n_new, tk)]
    return chunks


def _positions(q_start, ks, tq, tk):
    kpos = ks + lax.broadcasted_iota(jnp.int32, (tk, tq), 0)
    qpos = q_start + lax.broadcasted_iota(jnp.int32, (tk, tq), 1)
    return kpos, qpos


def _walk_key_blocks(step, q_start, tk, older):
    diag_start = (q_start // tk) * tk
    go = step(diag_start, tk, True)

    def cond(carry):
        piece, go = carry
        return jnp.logical_and(piece >= 0, go)

    def body(carry):
        piece, _ = carry
        return piece - 1, step(piece * older, older, False)

    lax.while_loop(cond, body, (diag_start // older - 1, go))


def _stage_lazily(stage, stage_static, chunks, off, tk, ks, masked):
    if masked:
        stage_static([c for c in chunks if c[3] >= off], off, tk)
    else:
        stage(0, ks, tk, 0, ks)
    return 0


def _pick_row(block, idx):
    row = lax.broadcasted_iota(jnp.int32, block.shape, 0)
    return jnp.sum(jnp.where(row == idx, block, 0.0), axis=0, keepdims=True)


def _gate_rows(f_row, key_side):
    n = f_row.shape[1]
    parts = [p.astype(F32) for p in _split3(-f_row if key_side else f_row)]
    ones = [jnp.ones((1, n), F32)] * 3
    row = lax.broadcasted_iota(jnp.int32, (SUBLANES, n), 0)
    out = jnp.zeros((SUBLANES, n), F32)
    for i, v in enumerate(parts + ones if key_side else ones + parts):
        out = jnp.where(row == i, v, out)
    return out


def _with_gate_rows(x_t, gates, head):
    n = x_t.shape[1]
    fill = jnp.zeros((HEAD_DIM - SUBLANES, n), F32)
    if head == 0:
        return jnp.concatenate([x_t[:HEAD_DIM], gates, fill], axis=0)
    return jnp.concatenate([gates, fill, x_t[HEAD_DIM:]], axis=0)


def _fox_kernel(*refs, tq, tk, off, n_heads, chunks, lazy):
    n_src = 1 + max(c[0] for c in chunks)
    bound_ref, fend_ref, q_ref = refs[:3]
    kv_refs = refs[3:3 + 2 * n_src]
    f_ref, o_ref, k_scr, vt_scr, m_scr, l_scr, acc_scr = refs[3 + 2 * n_src:]
    b = pl.program_id(0)
    group = pl.program_id(1)
    qi = pl.program_id(2)
    n_kb = f_ref.shape[2] // tk
    n_pairs = vt_scr.shape[0] // LANES
    first_head = 2 * n_pairs * group

    def stage(src, start, w, at, pos):
        k_ref, v_ref = kv_refs[2 * src], kv_refs[2 * src + 1]
        vt_scr[:, at:at + w] = v_ref[0, 0, :, pl.ds(start, w)].astype(BF16)
        f_blk = f_ref[0, :, pl.ds(pos, w)]
        for p in range(n_pairs):
            k_t = k_ref[0, 0, p * LANES:(p + 1) * LANES, pl.ds(start, w)]
            for j in range(2):
                gates = _gate_rows(_pick_row(f_blk, first_head + 2 * p + j), True)
                k_scr[2 * p + j, at:at + w, :] = _with_gate_rows(k_t, gates, j).T.astype(BF16)

    def stage_static(pieces, base, size):
        filled = max(dst + w for _, _, w, dst in pieces) - base
        if filled < size:
            k_scr[:, filled:size, :] = jnp.zeros((2 * n_pairs, size - filled, LANES), BF16)
            vt_scr[:, filled:size] = jnp.zeros((n_pairs * LANES, size - filled), BF16)
        for src, start, w, dst in pieces:
            stage(src, start, w, dst - base, dst)

    if not lazy:
        @pl.when(qi == 0)
        def _():
            stage_static(chunks, 0, n_kb * tk)

    q_start = off + qi * tq
    qs, fqs = [], []
    for p in range(n_pairs):
        q_t = q_ref[0, p * LANES:(p + 1) * LANES, :].astype(F32)
        for j in range(2):
            fq = _pick_row(f_ref[0, :, pl.ds(pl.multiple_of(q_start, LANES), tq)], first_head + 2 * p + j)
            fqs.append(fq)
            qs.append(_with_gate_rows(q_t, _gate_rows(fq, False), j).astype(BF16))

    m_scr[...] = jnp.full_like(m_scr, NEG)
    l_scr[...] = jnp.zeros_like(l_scr)
    acc_scr[...] = jnp.zeros_like(acc_scr)

    def step(ks, width, masked):
        assert width == tk
        ks = pl.multiple_of(ks, tk)
        kb = ks // tk
        if masked:
            kpos, qpos = _positions(q_start, ks, tq, tk)
            visible = kpos <= qpos
        at = _stage_lazily(stage, stage_static, chunks, off, tk, ks, masked) if lazy else ks
        heads = range(2 * n_pairs)
        scores = [_dot(k_scr[h, pl.ds(at, tk), :], qs[h]) for h in heads]
        go = False
        alphas, probs = [], []
        for h in heads:
            s = scores[h]
            if masked:
                s = jnp.where(visible, s, NEG)
            m_prev = m_scr[h]
            m_new = jnp.maximum(m_prev, jnp.max(s, axis=0, keepdims=True))
            alpha = jnp.exp2(m_prev - m_new)
            p = jnp.exp2(s - m_new)
            l_scr[h] = alpha * l_scr[h] + jnp.sum(p, axis=0, keepdims=True)
            m_scr[h] = m_new
            alphas.append(alpha)
            probs.append(p.astype(BF16))
            f_end = fend_ref[(b * n_heads + first_head + h) * n_kb + jnp.maximum(kb - 1, 0)]
            reach = jnp.max(fqs[h] - m_new) + bound_ref[0] - f_end
            go = jnp.logical_or(go, reach >= EXP2_ZERO_BELOW)
        pvs = [_dot(vt_scr[h * HEAD_DIM:(h + 1) * HEAD_DIM, pl.ds(at, tk)], probs[h]) for h in heads]
        for h in heads:
            acc_scr[h] = acc_scr[h] * alphas[h] + pvs[h]
        return go

    _walk_key_blocks(step, q_start, tk, tk)
    o_t = jnp.concatenate([acc_scr[h] / l_scr[h] for h in range(2 * n_pairs)], axis=0)
    o_ref[0] = o_t.T


def _sb_kernel(*refs, tq, tk, off, chunks, lazy):
    n_src = 1 + max(c[0] for c in chunks)
    q_ref = refs[0]
    kv_refs = refs[1:1 + 2 * n_src]
    tri_ref, o_ref, kb_scr, vt_scr, c_scr, acc_scr = refs[1 + 2 * n_src:]
    qi = pl.program_id(2)
    n_pairs = vt_scr.shape[0] // LANES
    row_lo = lax.broadcasted_iota(jnp.int32, (LANES, 1), 0) < HEAD_DIM

    def stage(src, start, w, at, pos):
        del pos
        k_ref, v_ref = kv_refs[2 * src], kv_refs[2 * src + 1]
        vt_scr[:, at:at + w] = v_ref[0, 0, :, pl.ds(start, w)].astype(BF16)
        for p in range(n_pairs):
            lanes = slice(p * LANES, (p + 1) * LANES)
            kb_scr[at:at + w, lanes] = k_ref[0, 0, lanes, pl.ds(start, w)].T.astype(BF16)

    def stage_static(pieces, base, size):
        filled = max(dst + w for _, _, w, dst in pieces) - base
        if filled < size:
            kb_scr[filled:size, :] = jnp.zeros((size - filled, n_pairs * LANES), BF16)
            vt_scr[:, filled:size] = jnp.zeros((n_pairs * LANES, size - filled), BF16)
        for src, start, w, dst in pieces:
            stage(src, start, w, dst - base, dst)

    if not lazy:
        @pl.when(qi == 0)
        def _():
            stage_static(chunks, 0, vt_scr.shape[1])

    qs = []
    for p in range(n_pairs):
        q_t = q_ref[0, p * LANES:(p + 1) * LANES, :]
        zero = jnp.zeros_like(q_t)
        qs += [jnp.where(row_lo, q_t, zero), jnp.where(row_lo, zero, q_t)]
    c_scr[...] = jnp.zeros_like(c_scr)
    acc_scr[...] = jnp.zeros_like(acc_scr)
    q_start = off + qi * tq

    def step(ks, tk, masked):
        ks = pl.multiple_of(ks, tk)
        if masked:
            kpos, qpos = _positions(q_start, ks, tq, tk)
            strict = kpos < qpos
        at = _stage_lazily(stage, stage_static, chunks, off, tk, ks, masked) if lazy else ks
        heads = range(2 * n_pairs)
        zs = [_dot(kb_scr[pl.ds(at, tk), (h // 2) * LANES:(h // 2 + 1) * LANES], qs[h])
              for h in heads]
        log_bs, log_1ms, c_olds = [], [], []
        reach = None
        for h in heads:
            z = zs[h]
            log_b = jnp.minimum(z, 0.0) - jnp.log2(1.0 + jnp.exp2(-jnp.abs(z)))
            log_1m = log_b - z
            if masked:
                log_1m = jnp.where(strict, log_1m, 0.0)
            c_old = c_scr[h]
            c_new = c_old + jnp.sum(log_1m, axis=0, keepdims=True)
            c_scr[h] = c_new
            reach = c_new if reach is None else jnp.maximum(reach, c_new)
            log_bs.append(log_b)
            log_1ms.append(log_1m.astype(BF16))
            c_olds.append(c_old)
        rests = [_dot(tri_ref[0:tk, 0:tk], log_1ms[h]) for h in heads]
        weights = []
        for h in heads:
            arg = log_bs[h] + (rests[h] + c_olds[h])
            if masked:
                arg = jnp.where(strict, arg, NEG)
            weights.append(jnp.exp2(arg).astype(BF16))
        pvs = [_dot(vt_scr[h * HEAD_DIM:(h + 1) * HEAD_DIM, pl.ds(at, tk)], weights[h]) for h in heads]
        for h in heads:
            acc_scr[h] = acc_scr[h] + pvs[h]
        return jnp.max(reach) >= EXP2_ZERO_BELOW

    _walk_key_blocks(step, q_start, tk, tk if lazy else OLDER_KEYS_SB)
    o_ref[0] = jnp.concatenate([acc_scr[h] for h in range(2 * n_pairs)], axis=0).T


def _attn_geometry(q_t, new, past, layer):
    b, width, tq_total = q_t.shape
    n_new = new[0].shape[3]
    n_past = 0 if past is None else past[0].shape[3]
    tk = KEY_BLOCK
    tq = 256 if tq_total % 256 == 0 else LANES
    assert tq_total % tq == 0 and tk % tq == 0 and n_past % tq == 0 and n_new >= tq_total
    chunks = _key_chunks(n_past, n_new, tk)
    tk_total = -(-(n_past + n_new) // tk) * tk
    gw = PAIRS_PER_STEP * LANES
    grid = (b, width // gw, tq_total // tq)
    lazy = tq_total == tq and n_past > 0 and n_new <= tk
    q_spec = pl.BlockSpec((1, gw, tq), lambda i, g, j: (i, g, j))
    o_spec = pl.BlockSpec((1, tq, gw), lambda i, g, j: (i, j, g))
    kv_ins, kv_specs = [], []
    for src in ([past] if past is not None else []) + [new]:
        n = src[0].shape[3]
        kv_ins += list(src)
        mode = {} if lazy else dict(pipeline_mode=pl.Buffered(1))
        kv_specs += [pl.BlockSpec((1, 1, gw, n), lambda i, g, j: (layer, i, g, 0), **mode)] * 2
    return dict(b=b, width=width, tq_total=tq_total, tq=tq, tk=tk, off=n_past, chunks=chunks, lazy=lazy,
                tk_total=tk_total, gw=gw, grid=grid, q_spec=q_spec, o_spec=o_spec,
                kv_ins=kv_ins, kv_specs=kv_specs)


def _fox_attention(q_t, new, past, layer, f2, qk_bound):
    G = _attn_geometry(q_t, new, past, layer)
    b, tq, tk, tk_total, gw = G["b"], G["tq"], G["tk"], G["tk_total"], G["gw"]
    n_heads = G["width"] // HEAD_DIM
    assert f2.shape == (b, n_heads, tk_total)
    f_end = f2[:, :, tk - 1::tk].reshape(-1)
    heads = gw // HEAD_DIM
    return pl.pallas_call(
        functools.partial(_fox_kernel, tq=tq, tk=tk, off=G["off"], n_heads=n_heads, chunks=G["chunks"],
                          lazy=G["lazy"]),
        grid=G["grid"],
        in_specs=[_SMEM, _SMEM, G["q_spec"]] + G["kv_specs"]
                 + [pl.BlockSpec((1, n_heads, tk_total), lambda i, g, j: (i, 0, 0))],
        out_specs=G["o_spec"],
        out_shape=jax.ShapeDtypeStruct((b, G["tq_total"], G["width"]), F32),
        scratch_shapes=[pltpu.VMEM((heads, tk if G["lazy"] else tk_total, LANES), BF16),
                        pltpu.VMEM((gw, tk if G["lazy"] else tk_total), BF16),
                        pltpu.VMEM((heads, 1, tq), F32), pltpu.VMEM((heads, 1, tq), F32),
                        pltpu.VMEM((heads, HEAD_DIM, tq), F32)],
        compiler_params=_params(3),
        name="fox_attention",
    )(qk_bound, f_end, q_t, *G["kv_ins"], f2)


def _sb_attention(q_t, new, past, layer):
    G = _attn_geometry(q_t, new, past, layer)
    b, tq, tk, tk_total, gw = G["b"], G["tq"], G["tk"], G["tk_total"], G["gw"]
    r = lax.broadcasted_iota(jnp.int32, (tk, tk), 0)
    c = lax.broadcasted_iota(jnp.int32, (tk, tk), 1)
    tri = (c > r).astype(BF16)
    heads = gw // HEAD_DIM
    return pl.pallas_call(
        functools.partial(_sb_kernel, tq=tq, tk=tk, off=G["off"], chunks=G["chunks"], lazy=G["lazy"]),
        grid=G["grid"],
        in_specs=[G["q_spec"]] + G["kv_specs"] + [_resident(tri)],
        out_specs=G["o_spec"],
        out_shape=jax.ShapeDtypeStruct((b, G["tq_total"], G["width"]), F32),
        scratch_shapes=[pltpu.VMEM((tk if G["lazy"] else tk_total, gw), BF16),
                        pltpu.VMEM((gw, tk if G["lazy"] else tk_total), BF16),
                        pltpu.VMEM((heads, 1, tq), F32), pltpu.VMEM((heads, HEAD_DIM, tq), F32)],
        compiler_params=_params(3),
        name="sb_attention",
    )(q_t, *G["kv_ins"], tri)


def _mix_out_cross_kernel(of_ref, os_ref, x_ref, mk_ref, mv_ref, gf_ref, gs_ref, w_ref,
                          g_ref, wq_ref, gq_ref, wo_ref, y_ref):
    a = _rms(of_ref[0], gf_ref[...]).astype(BF16)
    b = _rms(os_ref[0], gs_ref[...]).astype(BF16)
    nf = a.shape[1]
    x = x_ref[0] + _dot(a, w_ref[0:nf, :]) + _dot(b, w_ref[nf:, :])
    h = _rms(x, g_ref[...]).astype(BF16)
    hd = wq_ref.shape[1] // MEM_HEADS
    scale = hd ** -0.5
    heads = [slice(i * hd, (i + 1) * hd) for i in range(MEM_HEADS)]
    q_all = _dot(h, wq_ref[...])
    qs = [(_rms(q_all[:, sl], gq_ref[...]) * scale).astype(BF16) for sl in heads]
    scores = [_dot_nt(q, mk_ref[0, 0, :, sl].astype(BF16)) for q, sl in zip(qs, heads)]
    es = [jnp.exp(s - jnp.max(s, axis=1, keepdims=True)) for s in scores]
    pvs = [_dot(e.astype(BF16), mv_ref[0, 0, :, sl].astype(BF16)) for e, sl in zip(es, heads)]
    o = jnp.concatenate([(pv / jnp.sum(e, axis=1, keepdims=True)).astype(BF16)
                         for pv, e in zip(pvs, es)], axis=1)
    y_ref[0] = x + _dot(o, wo_ref[...])


def _mix_out_cross(o_f, o_s, x, mem_k, mem_v, layer, P):
    b, t, d = x.shape
    tm = min(512, t)
    row = lambda w: pl.BlockSpec((1, tm, w), lambda i, j: (i, j, 0))
    mem = pl.BlockSpec((1, 1) + mem_k.shape[2:], lambda i, j: (layer, i, 0, 0))
    ins = [o_f, o_s, x, mem_k, mem_v, P["g_out_fox"], P["g_out_sb"], P["w_out"],
           P["g_cross"], P["w_mq"], P["g_mq"], P["w_mo"]]
    return pl.pallas_call(
        _mix_out_cross_kernel,
        grid=(b, t // tm),
        in_specs=[row(o_f.shape[2]), row(o_s.shape[2]), row(d), mem, mem] + [_resident(a) for a in ins[5:]],
        out_specs=row(d),
        out_shape=jax.ShapeDtypeStruct((b, t, d), F32),
        compiler_params=_params(2),
        name="mix_out_cross",
    )(*ins)


def _ffn_kernel(x_ref, g_ref, w1_ref, w2_ref, y_ref, *, chunk):
    x = x_ref[...]
    h = _rms(x, g_ref[...]).astype(BF16)
    acc = x
    for c in range(w1_ref.shape[1] // chunk):
        sl = slice(c * chunk, (c + 1) * chunk)
        u = jnp.square(jnp.maximum(_dot(h, w1_ref[:, sl]), 0.0)).astype(BF16)
        acc = acc + _dot(u, w2_ref[sl, :])
    y_ref[...] = acc


def _ffn(x2d, P):
    n, d = x2d.shape
    tm = min(512, n)
    row = pl.BlockSpec((tm, d), lambda i: (i, 0))
    once = lambda a: pl.BlockSpec(a.shape, lambda i: (0, 0), pipeline_mode=pl.Buffered(1))
    return pl.pallas_call(
        functools.partial(_ffn_kernel, chunk=1024),
        grid=(n // tm,),
        in_specs=[row, _resident(P["g_ffn"]), once(P["w_ff1"]), once(P["w_ff2"])],
        out_specs=row,
        out_shape=jax.ShapeDtypeStruct((n, d), F32),
        compiler_params=_params(1),
        name="ffn",
    )(x2d, P["g_ffn"], P["w_ff1"], P["w_ff2"])


def _mem_kv_kernel(m_ref, g_ref, wk_ref, wv_ref, gk_ref, k_ref, v_ref):
    h = _rms(m_ref[...], g_ref[0]).astype(BF16)
    hd = wk_ref.shape[2] // MEM_HEADS
    for i in range(MEM_HEADS):
        sl = slice(i * hd, (i + 1) * hd)
        k_ref[0, :, sl] = _rms(_dot(h, wk_ref[0, :, sl]), gk_ref[0])
    v_ref[0] = _dot(h, wv_ref[0])


def _memory_kv(mem2d, g_mem, w_mk, w_mv, g_mk):
    n, d = mem2d.shape
    depth, _, w = w_mk.shape
    tm = min(256, n)
    per_layer = lambda a: pl.BlockSpec((1,) + a.shape[1:], lambda l, i: (l,) + (0,) * (a.ndim - 1))
    ins = [mem2d, g_mem.reshape(depth, 1, d).astype(F32), w_mk.astype(BF16), w_mv.astype(BF16),
           g_mk.reshape(depth, 1, -1).astype(F32)]
    out_spec = pl.BlockSpec((1, tm, w), lambda l, i: (l, i, 0))
    return pl.pallas_call(
        _mem_kv_kernel,
        grid=(depth, n // tm),
        in_specs=[pl.BlockSpec((tm, d), lambda l, i: (i, 0))] + [per_layer(a) for a in ins[1:]],
        out_specs=[out_spec, out_spec],
        out_shape=[jax.ShapeDtypeStruct((depth, n, w), F32)] * 2,
        compiler_params=_params(2),
        name="memory_kv",
    )(*ins)


def _layer(x, mem_k, mem_v, past, P, layer, depth, stacks, t_real):
    b, t, d = x.shape
    qf_t, qs_t, stacks = _proj_in(x, P, layer, depth, stacks)
    kf, vf, ks, vs, lf = stacks
    lf_t = lf[layer]
    if past is not None:
        lf_t = jnp.concatenate([past[2][layer], lf_t], axis=2)
    pad = -lf_t.shape[2] % KEY_BLOCK
    f2 = _cumsum_logf(jnp.pad(lf_t, ((0, 0), (0, 0), (0, pad))))
    o_f = _fox_attention(qf_t, (kf, vf), None if past is None else (past[0], past[1]), layer, f2, P["qk_bound"])
    o_s = _sb_attention(qs_t, (ks, vs), None if past is None else (past[3], past[4]), layer)
    y = _mix_out_cross(o_f[:, :t_real], o_s[:, :t_real], x[:, :t_real], mem_k, mem_v, layer, P)
    y = _ffn(y.reshape(b * t_real, d), P)
    return y.reshape(b, t_real, d), stacks


def _layer_params(l, g_mix, w_in, b_forget, g_fox_q, g_fox_k, g_out_fox, g_out_sb, w_out,
                  g_cross, g_mem, w_mq, w_mk, w_mv, g_mq, g_mk, w_mo, g_ffn, w_ff1, w_ff2):
    fox_w = g_out_fox.shape[1]
    sb_w = g_out_sb.shape[1]
    n_heads = b_forget.shape[1]
    assert fox_w == sb_w and n_heads == SUBLANES and g_fox_q.shape[1] == HEAD_DIM
    w = w_in[l]
    o_sb = 3 * fox_w + n_heads
    w_t = jnp.concatenate(
        [w[:, :3 * fox_w], w[:, o_sb:o_sb + 3 * sb_w],
         jnp.pad(w[:, 3 * fox_w:o_sb], ((0, 0), (0, 2 * SUBLANES - n_heads)))], axis=1).T.astype(BF16)
    row = lambda a: a[l].reshape(1, -1).astype(F32)
    col = lambda a: a[l].reshape(-1, 1).astype(F32)
    qk_bound = (LOG2E * HEAD_DIM ** 0.5 * 1.01) * jnp.max(jnp.abs(g_fox_q[l])) * jnp.max(jnp.abs(g_fox_k[l]))
    return dict(
        qk_bound=qk_bound.reshape(1).astype(F32),
        g_mix=row(g_mix), w_in_t=w_t,
        gq=jnp.tile(col(g_fox_q), (fox_w // HEAD_DIM, 1)),
        gk=jnp.tile(col(g_fox_k), (fox_w // HEAD_DIM, 1)),
        b_forget=col(b_forget),
        g_out_fox=row(g_out_fox), g_out_sb=row(g_out_sb), w_out=w_out[l].astype(BF16),
        g_cross=row(g_cross), w_mq=w_mq[l].astype(BF16), g_mq=row(g_mq), w_mo=w_mo[l].astype(BF16),
        g_ffn=row(g_ffn), w_ff1=w_ff1[l].astype(BF16), w_ff2=w_ff2[l].astype(BF16))


def _feature_major(cache):
    depth, b, t = cache.shape[:3]
    return jnp.moveaxis(cache.reshape(depth, b, t, -1), 2, 3)


def _time_major(stack, t, hd):
    depth, b, w, _ = stack.shape
    return jnp.moveaxis(stack[..., :t].reshape(depth, b, w // hd, hd, t), 4, 2)


def kernel(x_prompt, x_sample, mem_prompt, cache_fox_k, cache_fox_v, cache_fox_logf, cache_sb_k, cache_sb_v, cache_mem_k, cache_mem_v, g_mix, w_in, b_forget, g_fox_q, g_fox_k, g_out_fox, g_out_sb, w_out, g_cross, g_mem, w_mq, w_mk, w_mv, g_mq, g_mk, w_mo, g_ffn, w_ff1, w_ff2):
    depth = g_mix.shape[0]
    bp, tp, d = x_prompt.shape
    bs, ts, _ = x_sample.shape
    n_mem = mem_prompt.shape[1]
    n_heads = b_forget.shape[1]
    ts_pad = -(-ts // LANES) * LANES
    xp, xs = x_prompt, x_sample
    mem2d = mem_prompt.reshape(bp * n_mem, d)
    past = (_feature_major(cache_fox_k), _feature_major(cache_fox_v),
            jnp.moveaxis(cache_fox_logf.astype(F32), 2, 3),
            _feature_major(cache_sb_k), _feature_major(cache_sb_v))
    stacks_p = stacks_s = None
    p_mk, p_mv = [a.reshape(depth, bp, n_mem, -1) for a in _memory_kv(mem2d, g_mem, w_mk, w_mv, g_mk)]
    s_mk, s_mv = [a.reshape(depth, bs, n_mem, -1) for a in (cache_mem_k, cache_mem_v)]
    for l in range(depth):
        P = _layer_params(l, g_mix, w_in, b_forget, g_fox_q, g_fox_k, g_out_fox, g_out_sb, w_out,
                          g_cross, g_mem, w_mq, w_mk, w_mv, g_mq, g_mk, w_mo, g_ffn, w_ff1, w_ff2)
        xp, stacks_p = _layer(xp, p_mk, p_mv, None, P, l, depth, stacks_p, tp)
        xs_padded = jnp.pad(xs, ((0, 0), (0, ts_pad - ts), (0, 0)))
        xs, stacks_s = _layer(xs_padded, s_mk, s_mv, past, P, l, depth, stacks_s, ts)

    mem_hd = p_mk.shape[-1] // MEM_HEADS
    p_fk, p_fv, p_sk, p_sv, p_lf = stacks_p
    s_fk, s_fv, s_sk, s_sv, s_lf = stacks_s
    return (xp, xs,
            _time_major(p_fk, tp, HEAD_DIM), _time_major(p_fv, tp, HEAD_DIM),
            jnp.moveaxis(p_lf, 2, 3),
            _time_major(p_sk, tp, HEAD_DIM), _time_major(p_sv, tp, HEAD_DIM),
            p_mk.reshape(depth, bp, n_mem, MEM_HEADS, mem_hd),
            p_mv.reshape(depth, bp, n_mem, MEM_HEADS, mem_hd),
            _time_major(s_fk, ts, HEAD_DIM), _time_major(s_fv, ts, HEAD_DIM),
            jnp.moveaxis(s_lf[..., :ts], 2, 3),
            _time_major(s_sk, ts, HEAD_DIM), _time_major(s_sv, ts, HEAD_DIM))
```

```python
import functools
import math

import jax
import jax.numpy as jnp
from jax import lax
from jax.experimental import pallas as pl
from jax.experimental.pallas import tpu as pltpu

EPS = 1e-6
NEG = -1e30
HEAD_DIM = 64
MEM_HEADS = 4
LANES = 128
SUBLANES = 8
VMEM_LIMIT_BYTES = 56 * 2**20
LOG2E = math.log2(math.e)
EXP2_ZERO_BELOW = -106.0 * LOG2E
KEY_BLOCK = 256
OLDER_KEYS_SB = KEY_BLOCK
PAIRS_PER_STEP = 4
F32 = jnp.float32
BF16 = jnp.bfloat16


def _dot(a, b):
    return jnp.dot(a, b, preferred_element_type=F32)


def _dot_nt(a, b):
    return lax.dot_general(a, b, (((1,), (1,)), ((), ())), preferred_element_type=F32)


def _rms(x, g):
    return x * lax.rsqrt(jnp.mean(x * x, axis=-1, keepdims=True) + EPS) * g


def _split3(x):
    hi = x.astype(BF16)
    r = x - hi.astype(F32)
    mid = r.astype(BF16)
    lo = (r - mid.astype(F32)).astype(BF16)
    return hi, mid, lo


def _params(n_grid_dims):
    return pltpu.CompilerParams(
        dimension_semantics=("arbitrary",) * n_grid_dims,
        vmem_limit_bytes=VMEM_LIMIT_BYTES)


def _resident(arr):
    nd = arr.ndim
    return pl.BlockSpec(arr.shape, lambda *_: (0,) * nd)


_SMEM = pl.BlockSpec(memory_space=pltpu.SMEM)
_ANY = pl.BlockSpec(memory_space=pl.ANY)


def _proj_in_kernel(*refs, width, n_prev):
    x_ref, g_ref, wt_ref, gq_ref, gk_ref, bf_ref = refs[:6]
    qf_ref, qs_ref, kf_ref, vf_ref, ks_ref, vs_ref, lf_ref = refs[6 + n_prev:]
    ht = _rms(x_ref[0], g_ref[...]).T.astype(BF16)
    q_scale = LOG2E * HEAD_DIM ** -0.5

    def seg(i, rows=width):
        return _dot(wt_ref[i * width:i * width + rows, :], ht)

    def head_norm(y, g_col, store):
        for h in range(width // HEAD_DIM):
            rows = slice(h * HEAD_DIM, (h + 1) * HEAD_DIM)
            yh = y[rows, :]
            ms = jnp.mean(yh * yh, axis=0, keepdims=True)
            store(rows, yh * lax.rsqrt(ms + EPS) * g_col[rows, :])

    def store_qf(rows, y):
        qf_ref[0, rows, :] = (y * q_scale).astype(BF16)

    def store_kf(rows, y):
        kf_ref[0, 0, rows, :] = y

    head_norm(seg(0), gq_ref[...], store_qf)
    head_norm(seg(1), gk_ref[...], store_kf)
    vf_ref[0, 0] = seg(2)
    qs_ref[0] = (seg(3) * q_scale).astype(BF16)
    ks_ref[0, 0] = seg(4)
    vs_ref[0, 0] = seg(5)
    n_gates = lf_ref.shape[2]
    lf_ref[0, 0] = jax.nn.log_sigmoid(seg(6, 2 * SUBLANES)[0:n_gates, :] + bf_ref[...])


def _proj_in(x, P, layer, depth, prev):
    b, t, d = x.shape
    width = P["gq"].shape[0]
    n_gates = P["b_forget"].shape[0]
    tm = min(512, t)
    n_prev = 0 if prev is None else len(prev)
    ins = [x, P["g_mix"], P["w_in_t"], P["gq"], P["gk"], P["b_forget"]]
    q_spec = pl.BlockSpec((1, width, tm), lambda i, j: (i, 0, j))
    kv_spec = pl.BlockSpec((1, 1, width, tm), lambda i, j: (layer, i, 0, j))
    lf_spec = pl.BlockSpec((1, 1, n_gates, tm), lambda i, j: (layer, i, 0, j))
    sds = jax.ShapeDtypeStruct
    kv_shape = sds((depth, b, width, t), F32)
    outs = pl.pallas_call(
        functools.partial(_proj_in_kernel, width=width, n_prev=n_prev),
        grid=(b, t // tm),
        in_specs=[pl.BlockSpec((1, tm, d), lambda i, j: (i, j, 0))]
                 + [_resident(a) for a in ins[1:]] + [_ANY] * n_prev,
        out_specs=[q_spec, q_spec] + [kv_spec] * 4 + [lf_spec],
        out_shape=[sds((b, width, t), BF16)] * 2 + [kv_shape] * 4 + [sds((depth, b, n_gates, t), F32)],
        input_output_aliases={len(ins) + i: 2 + i for i in range(n_prev)},
        compiler_params=_params(2),
        name="proj_in",
    )(*ins, *(prev or ()))
    return outs[0], outs[1], tuple(outs[2:])


def _cumsum_kernel(lf_ref, tri_ref, f_ref, carry_ref):
    @pl.when(pl.program_id(0) == 0)
    def _():
        carry_ref[...] = jnp.zeros_like(carry_ref)

    tri = tri_ref[...]
    f = sum(_dot(part, tri) for part in _split3(lf_ref[...])) + carry_ref[...]
    f_ref[...] = f * LOG2E
    tb = f.shape[1]
    carry_ref[...] = f[:, tb - 1:tb]


def _cumsum_logf(lf_t):
    b, n_gates, t = lf_t.shape
    rows = b * n_gates
    tb = 512 if t % 512 == 0 else KEY_BLOCK
    r = lax.broadcasted_iota(jnp.int32, (tb, tb), 0)
    c = lax.broadcasted_iota(jnp.int32, (tb, tb), 1)
    tri = (r <= c).astype(BF16)
    spec = pl.BlockSpec((rows, tb), lambda j: (0, j))
    return pl.pallas_call(
        _cumsum_kernel,
        grid=(t // tb,),
        in_specs=[spec, _resident(tri)],
        out_specs=spec,
        out_shape=jax.ShapeDtypeStruct((rows, t), F32),
        scratch_shapes=[pltpu.VMEM((rows, 1), F32)],
        compiler_params=_params(1),
        name="cumsum_logf",
    )(lf_t.reshape(rows, t), tri).reshape(b, n_gates, t)


def _key_chunks(n_past, n_new, tk):
    assert n_past % tk == 0
    new_src = 1 if n_past else 0
    chunks = [(0, s, tk, s) for s in range(0, n_past, tk)]
    chunks += [(new_src, s, min(tk, n_new - s), n_past + s) for s in range(0, n_new, tk)]
    return chunks


def _positions(q_start, ks, tq, tk):
    kpos = ks + lax.broadcasted_iota(jnp.int32, (tk, tq), 0)
    qpos = q_start + lax.broadcasted_iota(jnp.int32, (tk, tq), 1)
    return kpos, qpos


def _walk_key_blocks(step, q_start, tk, older):
    diag_start = (q_start // tk) * tk
    go = step(diag_start, tk, True)

    def cond(carry):
        piece, go = carry
        return jnp.logical_and(piece >= 0, go)

    def body(carry):
        piece, _ = carry
        return piece - 1, step(piece * older, older, False)

    lax.while_loop(cond, body, (diag_start // older - 1, go))


def _stage_lazily(stage, stage_static, chunks, off, tk, ks, masked):
    if masked:
        stage_static([c for c in chunks if c[3] >= off], off, tk)
    else:
        stage(0, ks, tk, 0, ks)
    return 0


def _pick_row(block, idx):
    row = lax.broadcasted_iota(jnp.int32, block.shape, 0)
    return jnp.sum(jnp.where(row == idx, block, 0.0), axis=0, keepdims=True)


def _gate_rows(f_row, key_side):
    n = f_row.shape[1]
    parts = [p.astype(F32) for p in _split3(-f_row if key_side else f_row)]
    ones = [jnp.ones((1, n), F32)] * 3
    row = lax.broadcasted_iota(jnp.int32, (SUBLANES, n), 0)
    out = jnp.zeros((SUBLANES, n), F32)
    for i, v in enumerate(parts + ones if key_side else ones + parts):
        out = jnp.where(row == i, v, out)
    return out


def _with_gate_rows(x_t, gates, head):
    n = x_t.shape[1]
    fill = jnp.zeros((HEAD_DIM - SUBLANES, n), F32)
    if head == 0:
        return jnp.concatenate([x_t[:HEAD_DIM], gates, fill], axis=0)
    return jnp.concatenate([gates, fill, x_t[HEAD_DIM:]], axis=0)


def _fox_kernel(*refs, tq, tk, off, n_heads, chunks, lazy):
    n_src = 1 + max(c[0] for c in chunks)
    bound_ref, fend_ref, q_ref = refs[:3]
    kv_refs = refs[3:3 + 2 * n_src]
    f_ref, o_ref, k_scr, vt_scr, m_scr, l_scr, acc_scr = refs[3 + 2 * n_src:]
    b = pl.program_id(0)
    group = pl.program_id(1)
    qi = pl.program_id(2)
    n_kb = f_ref.shape[2] // tk
    n_pairs = vt_scr.shape[0] // LANES
    first_head = 2 * n_pairs * group

    def stage(src, start, w, at, pos):
        k_ref, v_ref = kv_refs[2 * src], kv_refs[2 * src + 1]
        vt_scr[:, at:at + w] = v_ref[0, 0, :, pl.ds(start, w)].astype(BF16)
        f_blk = f_ref[0, :, pl.ds(pos, w)]
        for p in range(n_pairs):
            k_t = k_ref[0, 0, p * LANES:(p + 1) * LANES, pl.ds(start, w)]
            for j in range(2):
                gates = _gate_rows(_pick_row(f_blk, first_head + 2 * p + j), True)
                k_scr[2 * p + j, at:at + w, :] = _with_gate_rows(k_t, gates, j).T.astype(BF16)

    def stage_static(pieces, base, size):
        filled = max(dst + w for _, _, w, dst in pieces) - base
        if filled < size:
            k_scr[:, filled:size, :] = jnp.zeros((2 * n_pairs, size - filled, LANES), BF16)
            vt_scr[:, filled:size] = jnp.zeros((n_pairs * LANES, size - filled), BF16)
        for src, start, w, dst in pieces:
            stage(src, start, w, dst - base, dst)

    if not lazy:
        @pl.when(qi == 0)
        def _():
            stage_static(chunks, 0, n_kb * tk)

    q_start = off + qi * tq
    qs, fqs = [], []
    for p in range(n_pairs):
        q_t = q_ref[0, p * LANES:(p + 1) * LANES, :].astype(F32)
        for j in range(2):
            fq = _pick_row(f_ref[0, :, pl.ds(pl.multiple_of(q_start, LANES), tq)], first_head + 2 * p + j)
            fqs.append(fq)
            qs.append(_with_gate_rows(q_t, _gate_rows(fq, False), j).astype(BF16))

    m_scr[...] = jnp.full_like(m_scr, NEG)
    l_scr[...] = jnp.zeros_like(l_scr)
    acc_scr[...] = jnp.zeros_like(acc_scr)

    def step(ks, width, masked):
        assert width == tk
        ks = pl.multiple_of(ks, tk)
        kb = ks // tk
        if masked:
            kpos, qpos = _positions(q_start, ks, tq, tk)
            visible = kpos <= qpos
        at = _stage_lazily(stage, stage_static, chunks, off, tk, ks, masked) if lazy else ks
        heads = range(2 * n_pairs)
        scores = [_dot(k_scr[h, pl.ds(at, tk), :], qs[h]) for h in heads]
        go = False
        alphas, probs = [], []
        for h in heads:
            s = scores[h]
            if masked:
                s = jnp.where(visible, s, NEG)
            m_prev = m_scr[h]
            m_new = jnp.maximum(m_prev, jnp.max(s, axis=0, keepdims=True))
            alpha = jnp.exp2(m_prev - m_new)
            p = jnp.exp2(s - m_new)
            l_scr[h] = alpha * l_scr[h] + jnp.sum(p, axis=0, keepdims=True)
            m_scr[h] = m_new
            alphas.append(alpha)
            probs.append(p.astype(BF16))
            f_end = fend_ref[(b * n_heads + first_head + h) * n_kb + jnp.maximum(kb - 1, 0)]
            reach = jnp.max(fqs[h] - m_new) + bound_ref[0] - f_end
            go = jnp.logical_or(go, reach >= EXP2_ZERO_BELOW)
        pvs = [_dot(vt_scr[h * HEAD_DIM:(h + 1) * HEAD_DIM, pl.ds(at, tk)], probs[h]) for h in heads]
        for h in heads:
            acc_scr[h] = acc_scr[h] * alphas[h] + pvs[h]
        return go

    _walk_key_blocks(step, q_start, tk, tk)
    o_t = jnp.concatenate([acc_scr[h] / l_scr[h] for h in range(2 * n_pairs)], axis=0)
    o_ref[0] = o_t.T


def _sb_kernel(*refs, tq, tk, off, chunks, lazy):
    n_src = 1 + max(c[0] for c in chunks)
    q_ref = refs[0]
    kv_refs = refs[1:1 + 2 * n_src]
    tri_ref, o_ref, kb_scr, vt_scr, c_scr, acc_scr = refs[1 + 2 * n_src:]
    qi = pl.program_id(2)
    n_pairs = vt_scr.shape[0] // LANES
    row_lo = lax.broadcasted_iota(jnp.int32, (LANES, 1), 0) < HEAD_DIM

    def stage(src, start, w, at, pos):
        del pos
        k_ref, v_ref = kv_refs[2 * src], kv_refs[2 * src + 1]
        vt_scr[:, at:at + w] = v_ref[0, 0, :, pl.ds(start, w)].astype(BF16)
        for p in range(n_pairs):
            lanes = slice(p * LANES, (p + 1) * LANES)
            kb_scr[at:at + w, lanes] = k_ref[0, 0, lanes, pl.ds(start, w)].T.astype(BF16)

    def stage_static(pieces, base, size):
        filled = max(dst + w for _, _, w, dst in pieces) - base
        if filled < size:
            kb_scr[filled:size, :] = jnp.zeros((size - filled, n_pairs * LANES), BF16)
            vt_scr[:, filled:size] = jnp.zeros((n_pairs * LANES, size - filled), BF16)
        for src, start, w, dst in pieces:
            stage(src, start, w, dst - base, dst)

    if not lazy:
        @pl.when(qi == 0)
        def _():
            stage_static(chunks, 0, vt_scr.shape[1])

    qs = []
    for p in range(n_pairs):
        q_t = q_ref[0, p * LANES:(p + 1) * LANES, :]
        zero = jnp.zeros_like(q_t)
        qs += [jnp.where(row_lo, q_t, zero), jnp.where(row_lo, zero, q_t)]
    c_scr[...] = jnp.zeros_like(c_scr)
    acc_scr[...] = jnp.zeros_like(acc_scr)
    q_start = off + qi * tq

    def step(ks, tk, masked):
        ks = pl.multiple_of(ks, tk)
        if masked:
            kpos, qpos = _positions(q_start, ks, tq, tk)
            strict = kpos < qpos
        at = _stage_lazily(stage, stage_static, chunks, off, tk, ks, masked) if lazy else ks
        heads = range(2 * n_pairs)
        zs = [_dot(kb_scr[pl.ds(at, tk), (h // 2) * LANES:(h // 2 + 1) * LANES], qs[h])
              for h in heads]
        log_bs, log_1ms, c_olds = [], [], []
        reach = None
        for h in heads:
            z = zs[h]
            log_b = jnp.minimum(z, 0.0) - jnp.log2(1.0 + jnp.exp2(-jnp.abs(z)))
            log_1m = log_b - z
            if masked:
                log_1m = jnp.where(strict, log_1m, 0.0)
            c_old = c_scr[h]
            c_new = c_old + jnp.sum(log_1m, axis=0, keepdims=True)
            c_scr[h] = c_new
            reach = c_new if reach is None else jnp.maximum(reach, c_new)
            log_bs.append(log_b)
            log_1ms.append(log_1m.astype(BF16))
            c_olds.append(c_old)
        rests = [_dot(tri_ref[0:tk, 0:tk], log_1ms[h]) for h in heads]
        weights = []
        for h in heads:
            arg = log_bs[h] + (rests[h] + c_olds[h])
            if masked:
                arg = jnp.where(strict, arg, NEG)
            weights.append(jnp.exp2(arg).astype(BF16))
        pvs = [_dot(vt_scr[h * HEAD_DIM:(h + 1) * HEAD_DIM, pl.ds(at, tk)], weights[h]) for h in heads]
        for h in heads:
            acc_scr[h] = acc_scr[h] + pvs[h]
        return jnp.max(reach) >= EXP2_ZERO_BELOW

    _walk_key_blocks(step, q_start, tk, tk if lazy else OLDER_KEYS_SB)
    o_ref[0] = jnp.concatenate([acc_scr[h] for h in range(2 * n_pairs)], axis=0).T


def _attn_geometry(q_t, new, past, layer):
    b, width, tq_total = q_t.shape
    n_new = new[0].shape[3]
    n_past = 0 if past is None else past[0].shape[3]
    tk = KEY_BLOCK
    tq = 256 if tq_total % 256 == 0 else LANES
    assert tq_total % tq == 0 and tk % tq == 0 and n_past % tq == 0 and n_new >= tq_total
    chunks = _key_chunks(n_past, n_new, tk)
    tk_total = -(-(n_past + n_new) // tk) * tk
    gw = PAIRS_PER_STEP * LANES
    grid = (b, width // gw, tq_total // tq)
    lazy = tq_total == tq and n_past > 0 and n_new <= tk
    q_spec = pl.BlockSpec((1, gw, tq), lambda i, g, j: (i, g, j))
    o_spec = pl.BlockSpec((1, tq, gw), lambda i, g, j: (i, j, g))
    kv_ins, kv_specs = [], []
    for src in ([past] if past is not None else []) + [new]:
        n = src[0].shape[3]
        kv_ins += list(src)
        mode = {} if lazy else dict(pipeline_mode=pl.Buffered(1))
        kv_specs += [pl.BlockSpec((1, 1, gw, n), lambda i, g, j: (layer, i, g, 0), **mode)] * 2
    return dict(b=b, width=width, tq_total=tq_total, tq=tq, tk=tk, off=n_past, chunks=chunks, lazy=lazy,
                tk_total=tk_total, gw=gw, grid=grid, q_spec=q_spec, o_spec=o_spec,
                kv_ins=kv_ins, kv_specs=kv_specs)


def _fox_attention(q_t, new, past, layer, f2, qk_bound):
    G = _attn_geometry(q_t, new, past, layer)
    b, tq, tk, tk_total, gw = G["b"], G["tq"], G["tk"], G["tk_total"], G["gw"]
    n_heads = G["width"] // HEAD_DIM
    assert f2.shape == (b, n_heads, tk_total)
    f_end = f2[:, :, tk - 1::tk].reshape(-1)
    heads = gw // HEAD_DIM
    return pl.pallas_call(
        functools.partial(_fox_kernel, tq=tq, tk=tk, off=G["off"], n_heads=n_heads, chunks=G["chunks"],
                          lazy=G["lazy"]),
        grid=G["grid"],
        in_specs=[_SMEM, _SMEM, G["q_spec"]] + G["kv_specs"]
                 + [pl.BlockSpec((1, n_heads, tk_total), lambda i, g, j: (i, 0, 0))],
        out_specs=G["o_spec"],
        out_shape=jax.ShapeDtypeStruct((b, G["tq_total"], G["width"]), F32),
        scratch_shapes=[pltpu.VMEM((heads, tk if G["lazy"] else tk_total, LANES), BF16),
                        pltpu.VMEM((gw, tk if G["lazy"] else tk_total), BF16),
                        pltpu.VMEM((heads, 1, tq), F32), pltpu.VMEM((heads, 1, tq), F32),
                        pltpu.VMEM((heads, HEAD_DIM, tq), F32)],
        compiler_params=_params(3),
        name="fox_attention",
    )(qk_bound, f_end, q_t, *G["kv_ins"], f2)


def _sb_attention(q_t, new, past, layer):
    G = _attn_geometry(q_t, new, past, layer)
    b, tq, tk, tk_total, gw = G["b"], G["tq"], G["tk"], G["tk_total"], G["gw"]
    r = lax.broadcasted_iota(jnp.int32, (tk, tk), 0)
    c = lax.broadcasted_iota(jnp.int32, (tk, tk), 1)
    tri = (c > r).astype(BF16)
    heads = gw // HEAD_DIM
    return pl.pallas_call(
        functools.partial(_sb_kernel, tq=tq, tk=tk, off=G["off"], chunks=G["chunks"], lazy=G["lazy"]),
        grid=G["grid"],
        in_specs=[G["q_spec"]] + G["kv_specs"] + [_resident(tri)],
        out_specs=G["o_spec"],
        out_shape=jax.ShapeDtypeStruct((b, G["tq_total"], G["width"]), F32),
        scratch_shapes=[pltpu.VMEM((tk if G["lazy"] else tk_total, gw), BF16),
                        pltpu.VMEM((gw, tk if G["lazy"] else tk_total), BF16),
                        pltpu.VMEM((heads, 1, tq), F32), pltpu.VMEM((heads, HEAD_DIM, tq), F32)],
        compiler_params=_params(3),
        name="sb_attention",
    )(q_t, *G["kv_ins"], tri)


def _mix_out_cross_kernel(of_ref, os_ref, x_ref, mk_ref, mv_ref, gf_ref, gs_ref, w_ref,
                          g_ref, wq_ref, gq_ref, wo_ref, y_ref):
    a = _rms(of_ref[0], gf_ref[...]).astype(BF16)
    b = _rms(os_ref[0], gs_ref[...]).astype(BF16)
    nf = a.shape[1]
    x = x_ref[0] + _dot(a, w_ref[0:nf, :]) + _dot(b, w_ref[nf:, :])
    h = _rms(x, g_ref[...]).astype(BF16)
    hd = wq_ref.shape[1] // MEM_HEADS
    scale = hd ** -0.5
    heads = [slice(i * hd, (i + 1) * hd) for i in range(MEM_HEADS)]
    q_all = _dot(h, wq_ref[...])
    qs = [(_rms(q_all[:, sl], gq_ref[...]) * scale).astype(BF16) for sl in heads]
    scores = [_dot_nt(q, mk_ref[0, 0, :, sl].astype(BF16)) for q, sl in zip(qs, heads)]
    es = [jnp.exp(s - jnp.max(s, axis=1, keepdims=True)) for s in scores]
    pvs = [_dot(e.astype(BF16), mv_ref[0, 0, :, sl].astype(BF16)) for e, sl in zip(es, heads)]
    o = jnp.concatenate([(pv / jnp.sum(e, axis=1, keepdims=True)).astype(BF16)
                         for pv, e in zip(pvs, es)], axis=1)
    y_ref[0] = x + _dot(o, wo_ref[...])


def _mix_out_cross(o_f, o_s, x, mem_k, mem_v, layer, P):
    b, t, d = x.shape
    tm = min(512, t)
    row = lambda w: pl.BlockSpec((1, tm, w), lambda i, j: (i, j, 0))
    mem = pl.BlockSpec((1, 1) + mem_k.shape[2:], lambda i, j: (layer, i, 0, 0))
    ins = [o_f, o_s, x, mem_k, mem_v, P["g_out_fox"], P["g_out_sb"], P["w_out"],
           P["g_cross"], P["w_mq"], P["g_mq"], P["w_mo"]]
    return pl.pallas_call(
        _mix_out_cross_kernel,
        grid=(b, t // tm),
        in_specs=[row(o_f.shape[2]), row(o_s.shape[2]), row(d), mem, mem] + [_resident(a) for a in ins[5:]],
        out_specs=row(d),
        out_shape=jax.ShapeDtypeStruct((b, t, d), F32),
        compiler_params=_params(2),
        name="mix_out_cross",
    )(*ins)


def _ffn_kernel(x_ref, g_ref, w1_ref, w2_ref, y_ref, *, chunk):
    x = x_ref[...]
    h = _rms(x, g_ref[...]).astype(BF16)
    acc = x
    for c in range(w1_ref.shape[1] // chunk):
        sl = slice(c * chunk, (c + 1) * chunk)
        u = jnp.square(jnp.maximum(_dot(h, w1_ref[:, sl]), 0.0)).astype(BF16)
        acc = acc + _dot(u, w2_ref[sl, :])
    y_ref[...] = acc


def _ffn(x2d, P):
    n, d = x2d.shape
    tm = min(512, n)
    row = pl.BlockSpec((tm, d), lambda i: (i, 0))
    once = lambda a: pl.BlockSpec(a.shape, lambda i: (0, 0), pipeline_mode=pl.Buffered(1))
    return pl.pallas_call(
        functools.partial(_ffn_kernel, chunk=1024),
        grid=(n // tm,),
        in_specs=[row, _resident(P["g_ffn"]), once(P["w_ff1"]), once(P["w_ff2"])],
        out_specs=row,
        out_shape=jax.ShapeDtypeStruct((n, d), F32),
        compiler_params=_params(1),
        name="ffn",
    )(x2d, P["g_ffn"], P["w_ff1"], P["w_ff2"])


def _mem_kv_kernel(m_ref, g_ref, wk_ref, wv_ref, gk_ref, k_ref, v_ref):
    h = _rms(m_ref[...], g_ref[0]).astype(BF16)
    hd = wk_ref.shape[2] // MEM_HEADS
    for i in range(MEM_HEADS):
        sl = slice(i * hd, (i + 1) * hd)
        k_ref[0, :, sl] = _rms(_dot(h, wk_ref[0, :, sl]), gk_ref[0])
    v_ref[0] = _dot(h, wv_ref[0])


def _memory_kv(mem2d, g_mem, w_mk, w_mv, g_mk):
    n, d = mem2d.shape
    depth, _, w = w_mk.shape
    tm = min(256, n)
    per_layer = lambda a: pl.BlockSpec((1,) + a.shape[1:], lambda l, i: (l,) + (0,) * (a.ndim - 1))
    ins = [mem2d, g_mem.reshape(depth, 1, d).astype(F32), w_mk.astype(BF16), w_mv.astype(BF16),
           g_mk.reshape(depth, 1, -1).astype(F32)]
    out_spec = pl.BlockSpec((1, tm, w), lambda l, i: (l, i, 0))
    return pl.pallas_call(
        _mem_kv_kernel,
        grid=(depth, n // tm),
        in_specs=[pl.BlockSpec((tm, d), lambda l, i: (i, 0))] + [per_layer(a) for a in ins[1:]],
        out_specs=[out_spec, out_spec],
        out_shape=[jax.ShapeDtypeStruct((depth, n, w), F32)] * 2,
        compiler_params=_params(2),
        name="memory_kv",
    )(*ins)


def _layer(x, mem_k, mem_v, past, P, layer, depth, stacks, t_real):
    b, t, d = x.shape
    qf_t, qs_t, stacks = _proj_in(x, P, layer, depth, stacks)
    kf, vf, ks, vs, lf = stacks
    lf_t = lf[layer]
    if past is not None:
        lf_t = jnp.concatenate([past[2][layer], lf_t], axis=2)
    pad = -lf_t.shape[2] % KEY_BLOCK
    f2 = _cumsum_logf(jnp.pad(lf_t, ((0, 0), (0, 0), (0, pad))))
    o_f = _fox_attention(qf_t, (kf, vf), None if past is None else (past[0], past[1]), layer, f2, P["qk_bound"])
    o_s = _sb_attention(qs_t, (ks, vs), None if past is None else (past[3], past[4]), layer)
    y = _mix_out_cross(o_f[:, :t_real], o_s[:, :t_real], x[:, :t_real], mem_k, mem_v, layer, P)
    y = _ffn(y.reshape(b * t_real, d), P)
    return y.reshape(b, t_real, d), stacks


def _layer_params(l, g_mix, w_in, b_forget, g_fox_q, g_fox_k, g_out_fox, g_out_sb, w_out,
                  g_cross, g_mem, w_mq, w_mk, w_mv, g_mq, g_mk, w_mo, g_ffn, w_ff1, w_ff2):
    fox_w = g_out_fox.shape[1]
    sb_w = g_out_sb.shape[1]
    n_heads = b_forget.shape[1]
    assert fox_w == sb_w and n_heads == SUBLANES and g_fox_q.shape[1] == HEAD_DIM
    w = w_in[l]
    o_sb = 3 * fox_w + n_heads
    w_t = jnp.concatenate(
        [w[:, :3 * fox_w], w[:, o_sb:o_sb + 3 * sb_w],
         jnp.pad(w[:, 3 * fox_w:o_sb], ((0, 0), (0, 2 * SUBLANES - n_heads)))], axis=1).T.astype(BF16)
    row = lambda a: a[l].reshape(1, -1).astype(F32)
    col = lambda a: a[l].reshape(-1, 1).astype(F32)
    qk_bound = (LOG2E * HEAD_DIM ** 0.5 * 1.01) * jnp.max(jnp.abs(g_fox_q[l])) * jnp.max(jnp.abs(g_fox_k[l]))
    return dict(
        qk_bound=qk_bound.reshape(1).astype(F32),
        g_mix=row(g_mix), w_in_t=w_t,
        gq=jnp.tile(col(g_fox_q), (fox_w // HEAD_DIM, 1)),
        gk=jnp.tile(col(g_fox_k), (fox_w // HEAD_DIM, 1)),
        b_forget=col(b_forget),
        g_out_fox=row(g_out_fox), g_out_sb=row(g_out_sb), w_out=w_out[l].astype(BF16),
        g_cross=row(g_cross), w_mq=w_mq[l].astype(BF16), g_mq=row(g_mq), w_mo=w_mo[l].astype(BF16),
        g_ffn=row(g_ffn), w_ff1=w_ff1[l].astype(BF16), w_ff2=w_ff2[l].astype(BF16))


def _feature_major(cache):
    depth, b, t = cache.shape[:3]
    return jnp.moveaxis(cache.reshape(depth, b, t, -1), 2, 3)


def _time_major(stack, t, hd):
    depth, b, w, _ = stack.shape
    return jnp.moveaxis(stack[..., :t].reshape(depth, b, w // hd, hd, t), 4, 2)


def kernel(x_prompt, x_sample, mem_prompt, cache_fox_k, cache_fox_v, cache_fox_logf, cache_sb_k, cache_sb_v, cache_mem_k, cache_mem_v, g_mix, w_in, b_forget, g_fox_q, g_fox_k, g_out_fox, g_out_sb, w_out, g_cross, g_mem, w_mq, w_mk, w_mv, g_mq, g_mk, w_mo, g_ffn, w_ff1, w_ff2):
    depth = g_mix.shape[0]
    bp, tp, d = x_prompt.shape
    bs, ts, _ = x_sample.shape
    n_mem = mem_prompt.shape[1]
    n_heads = b_forget.shape[1]
    ts_pad = -(-ts // LANES) * LANES
    xp, xs = x_prompt, x_sample
    mem2d = mem_prompt.reshape(bp * n_mem, d)
    past = (_feature_major(cache_fox_k), _feature_major(cache_fox_v),
            jnp.moveaxis(cache_fox_logf.astype(F32), 2, 3),
            _feature_major(cache_sb_k), _feature_major(cache_sb_v))
    stacks_p = stacks_s = None
    p_mk, p_mv = [a.reshape(depth, bp, n_mem, -1) for a in _memory_kv(mem2d, g_mem, w_mk, w_mv, g_mk)]
    s_mk, s_mv = [a.reshape(depth, bs, n_mem, -1) for a in (cache_mem_k, cache_mem_v)]
    for l in range(depth):
        P = _layer_params(l, g_mix, w_in, b_forget, g_fox_q, g_fox_k, g_out_fox, g_out_sb, w_out,
                          g_cross, g_mem, w_mq, w_mk, w_mv, g_mq, g_mk, w_mo, g_ffn, w_ff1, w_ff2)
        xp, stacks_p = _layer(xp, p_mk, p_mv, None, P, l, depth, stacks_p, tp)
        xs_padded = jnp.pad(xs, ((0, 0), (0, ts_pad - ts), (0, 0)))
        xs, stacks_s = _layer(xs_padded, s_mk, s_mv, past, P, l, depth, stacks_s, ts)

    mem_hd = p_mk.shape[-1] // MEM_HEADS
    p_fk, p_fv, p_sk, p_sv, p_lf = stacks_p
    s_fk, s_fv, s_sk, s_sv, s_lf = stacks_s
    return (xp, xs,
            _time_major(p_fk, tp, HEAD_DIM), _time_major(p_fv, tp, HEAD_DIM),
            jnp.moveaxis(p_lf, 2, 3),
            _time_major(p_sk, tp, HEAD_DIM), _time_major(p_sv, tp, HEAD_DIM),
            p_mk.reshape(depth, bp, n_mem, MEM_HEADS, mem_hd),
            p_mv.reshape(depth, bp, n_mem, MEM_HEADS, mem_hd),
            _time_major(s_fk, ts, HEAD_DIM), _time_major(s_fv, ts, HEAD_DIM),
            jnp.moveaxis(s_lf[..., :ts], 2, 3),
            _time_major(s_sk, ts, HEAD_DIM), _time_major(s_sv, ts, HEAD_DIM))
```

```python
import functools
import math

import jax
import jax.numpy as jnp
from jax import lax
from jax.experimental import pallas as pl
from jax.experimental.pallas import tpu as pltpu

EPS = 1e-6
NEG = -1e30
HEAD_DIM = 64
MEM_HEADS = 4
LANES = 128
SUBLANES = 8
VMEM_LIMIT_BYTES = 56 * 2**20
LOG2E = math.log2(math.e)
EXP2_ZERO_BELOW = -106.0 * LOG2E
KEY_BLOCK = 256
OLDER_KEYS_SB = KEY_BLOCK
PAIRS_PER_STEP = 4
F32 = jnp.float32
BF16 = jnp.bfloat16


def _dot(a, b):
    return jnp.dot(a, b, preferred_element_type=F32)


def _dot_nt(a, b):
    return lax.dot_general(a, b, (((1,), (1,)), ((), ())), preferred_element_type=F32)


def _dot_tn(a_t, b):
    return lax.dot_general(a_t, b, (((0,), (0,)), ((), ())), preferred_element_type=F32)


def _rms(x, g):
    return x * lax.rsqrt(jnp.mean(x * x, axis=-1, keepdims=True) + EPS) * g


def _split3(x):
    hi = x.astype(BF16)
    r = x - hi.astype(F32)
    mid = r.astype(BF16)
    lo = (r - mid.astype(F32)).astype(BF16)
    return hi, mid, lo


def _params(n_grid_dims):
    return pltpu.CompilerParams(
        dimension_semantics=("arbitrary",) * n_grid_dims,
        vmem_limit_bytes=VMEM_LIMIT_BYTES)


def _resident(arr):
    nd = arr.ndim
    return pl.BlockSpec(arr.shape, lambda *_: (0,) * nd)


_SMEM = pl.BlockSpec(memory_space=pltpu.SMEM)
_ANY = pl.BlockSpec(memory_space=pl.ANY)


def _proj_in_kernel(*refs, width, n_prev, own):
    x_ref, g_ref, wt_ref, gq_ref, gk_ref, bf_ref = refs[:6]
    qf_ref, qs_ref, kf_ref, vf_ref, ks_ref, vs_ref, lf_ref = refs[6 + n_prev:]
    ht = _rms(x_ref[0], g_ref[...]).T.astype(BF16)
    q_scale = LOG2E * HEAD_DIM ** -0.5

    def seg(i, rows=width):
        return _dot(wt_ref[i * width:i * width + rows, :], ht)

    def head_norm(y, g_col, store):
        for h in range(width // HEAD_DIM):
            rows = slice(h * HEAD_DIM, (h + 1) * HEAD_DIM)
            yh = y[rows, :]
            ms = jnp.mean(yh * yh, axis=0, keepdims=True)
            store(rows, yh * lax.rsqrt(ms + EPS) * g_col[rows, :])

    def store_qf(rows, y):
        qf_ref[0, rows, :] = (y * q_scale).astype(BF16)

    def store_kf(rows, y):
        kf_ref[own, 0, rows, :] = y

    head_norm(seg(0), gq_ref[...], store_qf)
    head_norm(seg(1), gk_ref[...], store_kf)
    vf_ref[own, 0] = seg(2)
    qs_ref[0] = (seg(3) * q_scale).astype(BF16)
    ks_ref[own, 0] = seg(4)
    vs_ref[own, 0] = seg(5)
    n_gates = lf_ref.shape[2]
    lf_ref[own, 0] = jax.nn.log_sigmoid(seg(6, 2 * SUBLANES)[0:n_gates, :] + bf_ref[...])
    for ref in (kf_ref, vf_ref, ks_ref, vs_ref, lf_ref):
        for other in range(ref.shape[0]):
            if other != own:
                ref[other, 0] = jnp.zeros(ref.shape[2:], ref.dtype)


def _proj_in(x, P, layer, depth, prev):
    b, t, d = x.shape
    width = P["gq"].shape[0]
    n_gates = P["b_forget"].shape[0]
    tm = min(512, t)
    n_prev = 0 if prev is None else len(prev)
    ins = [x, P["g_mix"], P["w_in_t"], P["gq"], P["gk"], P["b_forget"]]
    q_spec = pl.BlockSpec((1, width, tm), lambda i, j: (i, 0, j))
    layers, first = (depth, 0) if prev is None else (1, layer)
    kv_spec = pl.BlockSpec((layers, 1, width, tm), lambda i, j: (first, i, 0, j))
    lf_spec = pl.BlockSpec((layers, 1, n_gates, tm), lambda i, j: (first, i, 0, j))
    sds = jax.ShapeDtypeStruct
    kv_shape = sds((depth, b, width, t), F32)
    outs = pl.pallas_call(
        functools.partial(_proj_in_kernel, width=width, n_prev=n_prev, own=layer - first),
        grid=(b, t // tm),
        in_specs=[pl.BlockSpec((1, tm, d), lambda i, j: (i, j, 0))]
                 + [_resident(a) for a in ins[1:]] + [_ANY] * n_prev,
        out_specs=[q_spec, q_spec] + [kv_spec] * 4 + [lf_spec],
        out_shape=[sds((b, width, t), BF16)] * 2 + [kv_shape] * 4 + [sds((depth, b, n_gates, t), F32)],
        input_output_aliases={len(ins) + i: 2 + i for i in range(n_prev)},
        compiler_params=_params(2),
        name="proj_in",
    )(*ins, *(prev or ()))
    return outs[0], outs[1], tuple(outs[2:])


def _cumsum_kernel(lf_ref, tri_ref, f_ref, carry_ref):
    @pl.when(pl.program_id(0) == 0)
    def _():
        carry_ref[...] = jnp.zeros_like(carry_ref)

    tri = tri_ref[...]
    f = sum(_dot(part, tri) for part in _split3(lf_ref[...])) + carry_ref[...]
    f_ref[...] = f * LOG2E
    tb = f.shape[1]
    carry_ref[...] = f[:, tb - 1:tb]


def _cumsum_logf(lf_t):
    b, n_gates, t = lf_t.shape
    rows = b * n_gates
    tb = 512 if t % 512 == 0 else KEY_BLOCK
    r = lax.broadcasted_iota(jnp.int32, (tb, tb), 0)
    c = lax.broadcasted_iota(jnp.int32, (tb, tb), 1)
    tri = (r <= c).astype(BF16)
    spec = pl.BlockSpec((rows, tb), lambda j: (0, j))
    return pl.pallas_call(
        _cumsum_kernel,
        grid=(t // tb,),
        in_specs=[spec, _resident(tri)],
        out_specs=spec,
        out_shape=jax.ShapeDtypeStruct((rows, t), F32),
        scratch_shapes=[pltpu.VMEM((rows, 1), F32)],
        compiler_params=_params(1),
        name="cumsum_logf",
    )(lf_t.reshape(rows, t), tri).reshape(b, n_gates, t)


def _key_chunks(n_past, n_new, tk):
    assert n_past % tk == 0
    new_src = 1 if n_past else 0
    chunks = [(0, s, tk, s) for s in range(0, n_past, tk)]
    chunks += [(new_src, s, min(tk, n_new - s), n_past + s) for s in range(0, n_new, tk)]
    return chunks


def _positions(q_start, ks, tq, tk):
    kpos = ks + lax.broadcasted_iota(jnp.int32, (tk, tq), 0)
    qpos = q_start + lax.broadcasted_iota(jnp.int32, (tk, tq), 1)
    return kpos, qpos


def _walk_key_blocks(step, q_start, tk, older):
    diag_start = (q_start // tk) * tk
    go = step(diag_start, tk, True)

    def cond(carry):
        piece, go = carry
        return jnp.logical_and(piece >= 0, go)

    def body(carry):
        piece, _ = carry
        return piece - 1, step(piece * older, older, False)

    lax.while_loop(cond, body, (diag_start // older - 1, go))


def _stage_lazily(stage, stage_static, chunks, off, tk, ks, masked):
    if masked:
        stage_static([c for c in chunks if c[3] >= off], off, tk)
    else:
        stage(0, ks, tk, 0, ks)
    return 0


def _pick_row(block, idx):
    row = lax.broadcasted_iota(jnp.int32, block.shape, 0)
    return jnp.sum(jnp.where(row == idx, block, 0.0), axis=0, keepdims=True)


def _gate_rows(f_row, key_side):
    n = f_row.shape[1]
    parts = [p.astype(F32) for p in _split3(-f_row if key_side else f_row)]
    ones = [jnp.ones((1, n), F32)] * 3
    row = lax.broadcasted_iota(jnp.int32, (SUBLANES, n), 0)
    out = jnp.zeros((SUBLANES, n), F32)
    for i, v in enumerate(parts + ones if key_side else ones + parts):
        out = jnp.where(row == i, v, out)
    return out


def _with_gate_rows(x_t, gates, head):
    n = x_t.shape[1]
    fill = jnp.zeros((HEAD_DIM - SUBLANES, n), F32)
    if head == 0:
        return jnp.concatenate([x_t[:HEAD_DIM], gates, fill], axis=0)
    return jnp.concatenate([gates, fill, x_t[HEAD_DIM:]], axis=0)


def _fox_kernel(*refs, tq, tk, off, n_heads, chunks, lazy):
    n_src = 1 + max(c[0] for c in chunks)
    bound_ref, fend_ref, q_ref = refs[:3]
    kv_refs = refs[3:3 + 2 * n_src]
    f_ref, o_ref, k_scr, vt_scr, m_scr, l_scr, acc_scr = refs[3 + 2 * n_src:]
    b = pl.program_id(0)
    group = pl.program_id(1)
    qi = pl.program_id(2)
    n_kb = f_ref.shape[2] // tk
    n_pairs = vt_scr.shape[0] // LANES
    first_head = 2 * n_pairs * group

    def stage(src, start, w, at, pos):
        k_ref, v_ref = kv_refs[2 * src], kv_refs[2 * src + 1]
        vt_scr[:, at:at + w] = v_ref[0, 0, :, pl.ds(start, w)].astype(BF16)
        f_blk = f_ref[0, :, pl.ds(pos, w)]
        for p in range(n_pairs):
            k_t = k_ref[0, 0, p * LANES:(p + 1) * LANES, pl.ds(start, w)]
            for j in range(2):
                gates = _gate_rows(_pick_row(f_blk, first_head + 2 * p + j), True)
                k_scr[2 * p + j, at:at + w, :] = _with_gate_rows(k_t, gates, j).T.astype(BF16)

    def stage_static(pieces, base, size):
        filled = max(dst + w for _, _, w, dst in pieces) - base
        if filled < size:
            k_scr[:, filled:size, :] = jnp.zeros((2 * n_pairs, size - filled, LANES), BF16)
            vt_scr[:, filled:size] = jnp.zeros((n_pairs * LANES, size - filled), BF16)
        for src, start, w, dst in pieces:
            stage(src, start, w, dst - base, dst)

    if not lazy:
        @pl.when(qi == 0)
        def _():
            stage_static(chunks, 0, n_kb * tk)

    q_start = off + qi * tq
    qs, fqs = [], []
    for p in range(n_pairs):
        q_t = q_ref[0, p * LANES:(p + 1) * LANES, :].astype(F32)
        for j in range(2):
            fq = _pick_row(f_ref[0, :, pl.ds(pl.multiple_of(q_start, LANES), tq)], first_head + 2 * p + j)
            fqs.append(fq)
            qs.append(_with_gate_rows(q_t, _gate_rows(fq, False), j).astype(BF16))

    m_scr[...] = jnp.full_like(m_scr, NEG)
    l_scr[...] = jnp.zeros_like(l_scr)
    acc_scr[...] = jnp.zeros_like(acc_scr)

    def step(ks, width, masked):
        assert width == tk
        ks = pl.multiple_of(ks, tk)
        kb = ks // tk
        if masked:
            kpos, qpos = _positions(q_start, ks, tq, tk)
            visible = kpos <= qpos
        at = _stage_lazily(stage, stage_static, chunks, off, tk, ks, masked) if lazy else ks
        heads = range(2 * n_pairs)
        scores = [_dot(k_scr[h, pl.ds(at, tk), :], qs[h]) for h in heads]
        go = False
        alphas, probs = [], []
        for h in heads:
            s = scores[h]
            if masked:
                s = jnp.where(visible, s, NEG)
            m_prev = m_scr[h]
            m_new = jnp.maximum(m_prev, jnp.max(s, axis=0, keepdims=True))
            alpha = jnp.exp2(m_prev - m_new)
            p = jnp.exp2(s - m_new)
            l_scr[h] = alpha * l_scr[h] + jnp.sum(p, axis=0, keepdims=True)
            m_scr[h] = m_new
            alphas.append(alpha)
            probs.append(p.astype(BF16))
            f_end = fend_ref[(b * n_heads + first_head + h) * n_kb + jnp.maximum(kb - 1, 0)]
            reach = jnp.max(fqs[h] - m_new) + bound_ref[0] - f_end
            go = jnp.logical_or(go, reach >= EXP2_ZERO_BELOW)
        pvs = [_dot(vt_scr[h * HEAD_DIM:(h + 1) * HEAD_DIM, pl.ds(at, tk)], probs[h]) for h in heads]
        for h in heads:
            acc_scr[h] = acc_scr[h] * alphas[h] + pvs[h]
        return go

    _walk_key_blocks(step, q_start, tk, tk)
    o_ref[0] = jnp.concatenate([acc_scr[h] / l_scr[h] for h in range(2 * n_pairs)], axis=0)


def _sb_kernel(*refs, tq, tk, off, chunks, lazy):
    n_src = 1 + max(c[0] for c in chunks)
    q_ref = refs[0]
    kv_refs = refs[1:1 + 2 * n_src]
    tri_ref, o_ref, kb_scr, vt_scr, c_scr, acc_scr = refs[1 + 2 * n_src:]
    qi = pl.program_id(2)
    n_pairs = vt_scr.shape[0] // LANES
    row_lo = lax.broadcasted_iota(jnp.int32, (LANES, 1), 0) < HEAD_DIM

    def stage(src, start, w, at, pos):
        del pos
        k_ref, v_ref = kv_refs[2 * src], kv_refs[2 * src + 1]
        vt_scr[:, at:at + w] = v_ref[0, 0, :, pl.ds(start, w)].astype(BF16)
        for p in range(n_pairs):
            lanes = slice(p * LANES, (p + 1) * LANES)
            kb_scr[at:at + w, lanes] = k_ref[0, 0, lanes, pl.ds(start, w)].T.astype(BF16)

    def stage_static(pieces, base, size):
        filled = max(dst + w for _, _, w, dst in pieces) - base
        if filled < size:
            kb_scr[filled:size, :] = jnp.zeros((size - filled, n_pairs * LANES), BF16)
            vt_scr[:, filled:size] = jnp.zeros((n_pairs * LANES, size - filled), BF16)
        for src, start, w, dst in pieces:
            stage(src, start, w, dst - base, dst)

    if not lazy:
        @pl.when(qi == 0)
        def _():
            stage_static(chunks, 0, vt_scr.shape[1])

    qs = []
    for p in range(n_pairs):
        q_t = q_ref[0, p * LANES:(p + 1) * LANES, :]
        zero = jnp.zeros_like(q_t)
        qs += [jnp.where(row_lo, q_t, zero), jnp.where(row_lo, zero, q_t)]
    c_scr[...] = jnp.zeros_like(c_scr)
    acc_scr[...] = jnp.zeros_like(acc_scr)
    q_start = off + qi * tq

    def step(ks, tk, masked):
        ks = pl.multiple_of(ks, tk)
        if masked:
            kpos, qpos = _positions(q_start, ks, tq, tk)
            strict = kpos < qpos
        at = _stage_lazily(stage, stage_static, chunks, off, tk, ks, masked) if lazy else ks
        heads = range(2 * n_pairs)
        zs = [_dot(kb_scr[pl.ds(at, tk), (h // 2) * LANES:(h // 2 + 1) * LANES], qs[h])
              for h in heads]
        log_bs, log_1ms, c_olds = [], [], []
        reach = None
        for h in heads:
            z = zs[h]
            log_b = jnp.minimum(z, 0.0) - jnp.log2(1.0 + jnp.exp2(-jnp.abs(z)))
            log_1m = log_b - z
            if masked:
                log_1m = jnp.where(strict, log_1m, 0.0)
            c_old = c_scr[h]
            c_new = c_old + jnp.sum(log_1m, axis=0, keepdims=True)
            c_scr[h] = c_new
            reach = c_new if reach is None else jnp.maximum(reach, c_new)
            log_bs.append(log_b)
            log_1ms.append(log_1m.astype(BF16))
            c_olds.append(c_old)
        rests = [_dot(tri_ref[0:tk, 0:tk], log_1ms[h]) for h in heads]
        weights = []
        for h in heads:
            arg = log_bs[h] + (rests[h] + c_olds[h])
            if masked:
                arg = jnp.where(strict, arg, NEG)
            weights.append(jnp.exp2(arg).astype(BF16))
        pvs = [_dot(vt_scr[h * HEAD_DIM:(h + 1) * HEAD_DIM, pl.ds(at, tk)], weights[h]) for h in heads]
        for h in heads:
            acc_scr[h] = acc_scr[h] + pvs[h]
        return jnp.max(reach) >= EXP2_ZERO_BELOW

    _walk_key_blocks(step, q_start, tk, tk if lazy else OLDER_KEYS_SB)
    o_ref[0] = jnp.concatenate([acc_scr[h] for h in range(2 * n_pairs)], axis=0)


def _attn_geometry(q_t, new, past, layer):
    b, width, tq_total = q_t.shape
    n_new = new[0].shape[3]
    n_past = 0 if past is None else past[0].shape[3]
    tk = KEY_BLOCK
    tq = 256 if tq_total % 256 == 0 else LANES
    assert tq_total % tq == 0 and tk % tq == 0 and n_past % tq == 0 and n_new >= tq_total
    chunks = _key_chunks(n_past, n_new, tk)
    tk_total = -(-(n_past + n_new) // tk) * tk
    gw = PAIRS_PER_STEP * LANES
    grid = (b, width // gw, tq_total // tq)
    lazy = tq_total == tq and n_past > 0 and n_new <= tk
    q_spec = pl.BlockSpec((1, gw, tq), lambda i, g, j: (i, g, j))
    o_spec = pl.BlockSpec((1, gw, tq), lambda i, g, j: (i, g, j))
    kv_ins, kv_specs = [], []
    for src in ([past] if past is not None else []) + [new]:
        n = src[0].shape[3]
        kv_ins += list(src)
        mode = {} if lazy else dict(pipeline_mode=pl.Buffered(1))
        kv_specs += [pl.BlockSpec((1, 1, gw, n), lambda i, g, j: (layer, i, g, 0), **mode)] * 2
    return dict(b=b, width=width, tq_total=tq_total, tq=tq, tk=tk, off=n_past, chunks=chunks, lazy=lazy,
                tk_total=tk_total, gw=gw, grid=grid, q_spec=q_spec, o_spec=o_spec,
                kv_ins=kv_ins, kv_specs=kv_specs)


def _fox_attention(q_t, new, past, layer, f2, qk_bound):
    G = _attn_geometry(q_t, new, past, layer)
    b, tq, tk, tk_total, gw = G["b"], G["tq"], G["tk"], G["tk_total"], G["gw"]
    n_heads = G["width"] // HEAD_DIM
    assert f2.shape == (b, n_heads, tk_total)
    f_end = lax.cummin(jnp.min(f2.reshape(b, n_heads, tk_total // tk, tk), axis=3), axis=2).reshape(-1)
    heads = gw // HEAD_DIM
    return pl.pallas_call(
        functools.partial(_fox_kernel, tq=tq, tk=tk, off=G["off"], n_heads=n_heads, chunks=G["chunks"],
                          lazy=G["lazy"]),
        grid=G["grid"],
        in_specs=[_SMEM, _SMEM, G["q_spec"]] + G["kv_specs"]
                 + [pl.BlockSpec((1, n_heads, tk_total), lambda i, g, j: (i, 0, 0))],
        out_specs=G["o_spec"],
        out_shape=jax.ShapeDtypeStruct((b, G["width"], G["tq_total"]), F32),
        scratch_shapes=[pltpu.VMEM((heads, tk if G["lazy"] else tk_total, LANES), BF16),
                        pltpu.VMEM((gw, tk if G["lazy"] else tk_total), BF16),
                        pltpu.VMEM((heads, 1, tq), F32), pltpu.VMEM((heads, 1, tq), F32),
                        pltpu.VMEM((heads, HEAD_DIM, tq), F32)],
        compiler_params=_params(3),
        name="fox_attention",
    )(qk_bound, f_end, q_t, *G["kv_ins"], f2)


def _sb_attention(q_t, new, past, layer):
    G = _attn_geometry(q_t, new, past, layer)
    b, tq, tk, tk_total, gw = G["b"], G["tq"], G["tk"], G["tk_total"], G["gw"]
    r = lax.broadcasted_iota(jnp.int32, (tk, tk), 0)
    c = lax.broadcasted_iota(jnp.int32, (tk, tk), 1)
    tri = (c > r).astype(BF16)
    heads = gw // HEAD_DIM
    return pl.pallas_call(
        functools.partial(_sb_kernel, tq=tq, tk=tk, off=G["off"], chunks=G["chunks"], lazy=G["lazy"]),
        grid=G["grid"],
        in_specs=[G["q_spec"]] + G["kv_specs"] + [_resident(tri)],
        out_specs=G["o_spec"],
        out_shape=jax.ShapeDtypeStruct((b, G["width"], G["tq_total"]), F32),
        scratch_shapes=[pltpu.VMEM((tk if G["lazy"] else tk_total, gw), BF16),
                        pltpu.VMEM((gw, tk if G["lazy"] else tk_total), BF16),
                        pltpu.VMEM((heads, 1, tq), F32), pltpu.VMEM((heads, HEAD_DIM, tq), F32)],
        compiler_params=_params(3),
        name="sb_attention",
    )(q_t, *G["kv_ins"], tri)


def _mix_out_cross_kernel(of_ref, os_ref, x_ref, mk_ref, mv_ref, gf_ref, gs_ref, w_ref,
                          g_ref, wq_ref, gq_ref, wo_ref, y_ref):
    def normed_t(o_ref, g_col):
        o_t = o_ref[0]
        return (o_t * lax.rsqrt(jnp.mean(o_t * o_t, axis=0, keepdims=True) + EPS) * g_col).astype(BF16)

    a_t = normed_t(of_ref, gf_ref[...])
    b_t = normed_t(os_ref, gs_ref[...])
    nf = a_t.shape[0]
    x = x_ref[0] + _dot_tn(a_t, w_ref[0:nf, :]) + _dot_tn(b_t, w_ref[nf:, :])
    h = _rms(x, g_ref[...]).astype(BF16)
    hd = wq_ref.shape[1] // MEM_HEADS
    scale = hd ** -0.5
    heads = [slice(i * hd, (i + 1) * hd) for i in range(MEM_HEADS)]
    q_all = _dot(h, wq_ref[...])
    qs = [(_rms(q_all[:, sl], gq_ref[...]) * scale).astype(BF16) for sl in heads]
    scores = [_dot_nt(q, mk_ref[0, 0, :, sl].astype(BF16)) for q, sl in zip(qs, heads)]
    es = [jnp.exp(s - jnp.max(s, axis=1, keepdims=True)) for s in scores]
    pvs = [_dot(e.astype(BF16), mv_ref[0, 0, :, sl].astype(BF16)) for e, sl in zip(es, heads)]
    o = jnp.concatenate([(pv / jnp.sum(e, axis=1, keepdims=True)).astype(BF16)
                         for pv, e in zip(pvs, es)], axis=1)
    y_ref[0] = x + _dot(o, wo_ref[...])


def _mix_out_cross(o_f, o_s, x, mem_k, mem_v, layer, P):
    b, t, d = x.shape
    tm = min(512, t)
    row = lambda w: pl.BlockSpec((1, tm, w), lambda i, j: (i, j, 0))
    col = lambda w: pl.BlockSpec((1, w, tm), lambda i, j: (i, 0, j))
    mem = pl.BlockSpec((1, 1) + mem_k.shape[2:], lambda i, j: (layer, i, 0, 0))
    ins = [o_f, o_s, x, mem_k, mem_v, P["g_out_fox"], P["g_out_sb"], P["w_out"],
           P["g_cross"], P["w_mq"], P["g_mq"], P["w_mo"]]
    return pl.pallas_call(
        _mix_out_cross_kernel,
        grid=(b, t // tm),
        in_specs=[col(o_f.shape[1]), col(o_s.shape[1]), row(d), mem, mem] + [_resident(a) for a in ins[5:]],
        out_specs=row(d),
        out_shape=jax.ShapeDtypeStruct((b, t, d), F32),
        compiler_params=_params(2),
        name="mix_out_cross",
    )(*ins)


def _ffn_kernel(x_ref, g_ref, w1_ref, w2_ref, y_ref, *, chunk):
    x = x_ref[...]
    h = _rms(x, g_ref[...]).astype(BF16)
    acc = x
    for c in range(w1_ref.shape[1] // chunk):
        sl = slice(c * chunk, (c + 1) * chunk)
        u = jnp.square(jnp.maximum(_dot(h, w1_ref[:, sl]), 0.0)).astype(BF16)
        acc = acc + _dot(u, w2_ref[sl, :])
    y_ref[...] = acc


def _ffn(x2d, P):
    n, d = x2d.shape
    tm = min(1024, n)
    row = pl.BlockSpec((tm, d), lambda i: (i, 0))
    once = lambda a: pl.BlockSpec(a.shape, lambda i: (0, 0), pipeline_mode=pl.Buffered(1))
    return pl.pallas_call(
        functools.partial(_ffn_kernel, chunk=1024),
        grid=(n // tm,),
        in_specs=[row, _resident(P["g_ffn"]), once(P["w_ff1"]), once(P["w_ff2"])],
        out_specs=row,
        out_shape=jax.ShapeDtypeStruct((n, d), F32),
        compiler_params=_params(1),
        name="ffn",
    )(x2d, P["g_ffn"], P["w_ff1"], P["w_ff2"])


def _mem_kv_kernel(m_ref, g_ref, wk_ref, wv_ref, gk_ref, k_ref, v_ref):
    h = _rms(m_ref[...], g_ref[0]).astype(BF16)
    hd = wk_ref.shape[2] // MEM_HEADS
    for i in range(MEM_HEADS):
        sl = slice(i * hd, (i + 1) * hd)
        k_ref[0, :, sl] = _rms(_dot(h, wk_ref[0, :, sl]), gk_ref[0])
    v_ref[0] = _dot(h, wv_ref[0])


def _memory_kv(mem2d, g_mem, w_mk, w_mv, g_mk):
    n, d = mem2d.shape
    depth, _, w = w_mk.shape
    tm = min(256, n)
    per_layer = lambda a: pl.BlockSpec((1,) + a.shape[1:], lambda l, i: (l,) + (0,) * (a.ndim - 1))
    ins = [mem2d, g_mem.reshape(depth, 1, d).astype(F32), w_mk.astype(BF16), w_mv.astype(BF16),
           g_mk.reshape(depth, 1, -1).astype(F32)]
    out_spec = pl.BlockSpec((1, tm, w), lambda l, i: (l, i, 0))
    return pl.pallas_call(
        _mem_kv_kernel,
        grid=(depth, n // tm),
        in_specs=[pl.BlockSpec((tm, d), lambda l, i: (i, 0))] + [per_layer(a) for a in ins[1:]],
        out_specs=[out_spec, out_spec],
        out_shape=[jax.ShapeDtypeStruct((depth, n, w), F32)] * 2,
        compiler_params=_params(2),
        name="memory_kv",
    )(*ins)


def _layer(x, mem_k, mem_v, past, P, layer, depth, stacks, t_real):
    b, t, d = x.shape
    qf_t, qs_t, stacks = _proj_in(x, P, layer, depth, stacks)
    kf, vf, ks, vs, lf = stacks
    lf_t = lf[layer]
    if past is not None:
        lf_t = jnp.concatenate([past[2][layer], lf_t], axis=2)
    pad = -lf_t.shape[2] % KEY_BLOCK
    f2 = _cumsum_logf(jnp.pad(lf_t, ((0, 0), (0, 0), (0, pad))))
    o_f = _fox_attention(qf_t, (kf, vf), None if past is None else (past[0], past[1]), layer, f2, P["qk_bound"])
    o_s = _sb_attention(qs_t, (ks, vs), None if past is None else (past[3], past[4]), layer)
    y = _mix_out_cross(o_f[:, :, :t_real], o_s[:, :, :t_real], x[:, :t_real], mem_k, mem_v, layer, P)
    y = _ffn(y.reshape(b * t_real, d), P)
    return y.reshape(b, t_real, d), stacks


def _layer_params(l, g_mix, w_in, b_forget, g_fox_q, g_fox_k, g_out_fox, g_out_sb, w_out,
                  g_cross, g_mem, w_mq, w_mk, w_mv, g_mq, g_mk, w_mo, g_ffn, w_ff1, w_ff2):
    fox_w = g_out_fox.shape[1]
    sb_w = g_out_sb.shape[1]
    n_heads = b_forget.shape[1]
    assert fox_w == sb_w and n_heads == SUBLANES and g_fox_q.shape[1] == HEAD_DIM
    w = w_in[l]
    o_sb = 3 * fox_w + n_heads
    w_t = jnp.concatenate(
        [w[:, :3 * fox_w], w[:, o_sb:o_sb + 3 * sb_w],
         jnp.pad(w[:, 3 * fox_w:o_sb], ((0, 0), (0, 2 * SUBLANES - n_heads)))], axis=1).T.astype(BF16)
    row = lambda a: a[l].reshape(1, -1).astype(F32)
    col = lambda a: a[l].reshape(-1, 1).astype(F32)
    qk_bound = (LOG2E * HEAD_DIM ** 0.5 * 1.01) * jnp.max(jnp.abs(g_fox_q[l])) * jnp.max(jnp.abs(g_fox_k[l]))
    return dict(
        qk_bound=qk_bound.reshape(1).astype(F32),
        g_mix=row(g_mix), w_in_t=w_t,
        gq=jnp.tile(col(g_fox_q), (fox_w // HEAD_DIM, 1)),
        gk=jnp.tile(col(g_fox_k), (fox_w // HEAD_DIM, 1)),
        b_forget=col(b_forget),
        g_out_fox=col(g_out_fox), g_out_sb=col(g_out_sb), w_out=w_out[l].astype(BF16),
        g_cross=row(g_cross), w_mq=w_mq[l].astype(BF16), g_mq=row(g_mq), w_mo=w_mo[l].astype(BF16),
        g_ffn=row(g_ffn), w_ff1=w_ff1[l].astype(BF16), w_ff2=w_ff2[l].astype(BF16))


def _feature_major(cache):
    depth, b, t = cache.shape[:3]
    return jnp.moveaxis(cache.reshape(depth, b, t, -1), 2, 3)


def _time_major(stack, t, hd):
    depth, b, w, _ = stack.shape
    return jnp.moveaxis(stack[..., :t].reshape(depth, b, w // hd, hd, t), 4, 2)


def kernel(x_prompt, x_sample, mem_prompt, cache_fox_k, cache_fox_v, cache_fox_logf, cache_sb_k, cache_sb_v, cache_mem_k, cache_mem_v, g_mix, w_in, b_forget, g_fox_q, g_fox_k, g_out_fox, g_out_sb, w_out, g_cross, g_mem, w_mq, w_mk, w_mv, g_mq, g_mk, w_mo, g_ffn, w_ff1, w_ff2):
    depth = g_mix.shape[0]
    bp, tp, d = x_prompt.shape
    bs, ts, _ = x_sample.shape
    n_mem = mem_prompt.shape[1]
    n_heads = b_forget.shape[1]
    ts_pad = -(-ts // LANES) * LANES
    xp, xs = x_prompt, x_sample
    mem2d = mem_prompt.reshape(bp * n_mem, d)
    past = (_feature_major(cache_fox_k), _feature_major(cache_fox_v),
            jnp.moveaxis(cache_fox_logf.astype(F32), 2, 3),
            _feature_major(cache_sb_k), _feature_major(cache_sb_v))
    stacks_p = stacks_s = None
    p_mk, p_mv = [a.reshape(depth, bp, n_mem, -1) for a in _memory_kv(mem2d, g_mem, w_mk, w_mv, g_mk)]
    s_mk, s_mv = [a.reshape(depth, bs, n_mem, -1) for a in (cache_mem_k, cache_mem_v)]
    for l in range(depth):
        P = _layer_params(l, g_mix, w_in, b_forget, g_fox_q, g_fox_k, g_out_fox, g_out_sb, w_out,
                          g_cross, g_mem, w_mq, w_mk, w_mv, g_mq, g_mk, w_mo, g_ffn, w_ff1, w_ff2)
        xp, stacks_p = _layer(xp, p_mk, p_mv, None, P, l, depth, stacks_p, tp)
        xs_padded = jnp.pad(xs, ((0, 0), (0, ts_pad - ts), (0, 0)))
        xs, stacks_s = _layer(xs_padded, s_mk, s_mv, past, P, l, depth, stacks_s, ts)

    mem_hd = p_mk.shape[-1] // MEM_HEADS
    p_fk, p_fv, p_sk, p_sv, p_lf = stacks_p
    s_fk, s_fv, s_sk, s_sv, s_lf = stacks_s
    return (xp, xs,
            _time_major(p_fk, tp, HEAD_DIM), _time_major(p_fv, tp, HEAD_DIM),
            jnp.moveaxis(p_lf, 2, 3),
            _time_major(p_sk, tp, HEAD_DIM), _time_major(p_sv, tp, HEAD_DIM),
            p_mk.reshape(depth, bp, n_mem, MEM_HEADS, mem_hd),
            p_mv.reshape(depth, bp, n_mem, MEM_HEADS, mem_hd),
            _time_major(s_fk, ts, HEAD_DIM), _time_major(s_fv, ts, HEAD_DIM),
            jnp.moveaxis(s_lf[..., :ts], 2, 3),
            _time_major(s_sk, ts, HEAD_DIM), _time_major(s_sv, ts, HEAD_DIM))
```
